```python
import math
import jax, jax.numpy as jnp
from jax import lax
import numpy as np

D_MODEL = 1024
BATCH = 16
SEQ = 4096
DEPTH = 4

HEAD_DIM = 64
ROT_DIM = HEAD_DIM // 4
ROPE_THETA = 500000.0
NORM_EPS = 1e-6
Q_BLOCK = 128

A_HEADS = 8
A_PATTERNS = ((128, 1), (512, 4), (2048, 16))
A_W = A_HEADS * HEAD_DIM
B_HEADS = 4
B_VDIM = 2 * HEAD_DIM
B_W = B_HEADS * B_VDIM
C_HEADS = 16
C_KV_HEADS = 4
IDX_HEADS = 8
IDX_DIM = 64
TOPK_MAX = 256
D_FF = 4 * D_MODEL

N_EVEN = (DEPTH + 1) // 2
N_ODD = DEPTH // 2
EVEN_IN = 3 * A_W + 3 * B_W
EVEN_SPLITS = [A_W, 2 * A_W, 3 * A_W, 3 * A_W + B_W, 3 * A_W + 2 * B_W]
C_QW = C_HEADS * HEAD_DIM
C_KVW = C_KV_HEADS * HEAD_DIM
ODD_IN = C_QW + 2 * C_KVW + IDX_HEADS * IDX_DIM + IDX_DIM + IDX_HEADS
ODD_SPLITS = [C_QW, C_QW + C_KVW, C_QW + 2 * C_KVW,
              C_QW + 2 * C_KVW + IDX_HEADS * IDX_DIM,
              C_QW + 2 * C_KVW + IDX_HEADS * IDX_DIM + IDX_DIM]

kernel_name = "hybrid_dilated_diff_dsa_trunk"


def _rmsnorm(x, g):
    xf = x.astype(jnp.float32)
    y = xf * lax.rsqrt(jnp.mean(xf * xf, axis=-1, keepdims=True) + NORM_EPS)
    return (y * g.astype(jnp.float32)).astype(x.dtype)


def _rope_tables(seq_len):
    pos = jnp.arange(seq_len, dtype=jnp.float32)
    inv_freq = jnp.power(ROPE_THETA, -jnp.arange(0, ROT_DIM, 2, dtype=jnp.float32) / ROT_DIM)
    ang = pos[:, None] * inv_freq[None, :]
    return jnp.cos(ang), jnp.sin(ang)


def _partial_rope(x, cos, sin):
    half = ROT_DIM // 2
    c = cos[:, None, :].astype(x.dtype)
    s = sin[:, None, :].astype(x.dtype)
    x1 = x[..., :half]
    x2 = x[..., half:ROT_DIM]
    return jnp.concatenate([x1 * c - x2 * s, x2 * c + x1 * s, x[..., ROT_DIM:]], axis=-1)


def _dilated_branch(q, k, v, window, dil):
    B, S, H, Dh = q.shape
    ls = S // dil
    blk = window // dil
    nb = -(-ls // blk)
    lp = nb * blk

    def to_blocks(t):
        t = t.reshape(B, ls, dil, H, Dh).transpose(0, 2, 1, 3, 4)
        t = jnp.pad(t, ((0, 0), (0, 0), (0, lp - ls), (0, 0), (0, 0)))
        return t.reshape(B, dil, nb, blk, H, Dh)

    def with_prev(t):
        prev = jnp.pad(t[:, :, :-1], ((0, 0), (0, 0), (1, 0), (0, 0), (0, 0), (0, 0)))
        return jnp.concatenate([prev, t], axis=3)

    qb = to_blocks(q)
    kc = with_prev(to_blocks(k))
    vc = with_prev(to_blocks(v))
    s = jnp.einsum('brnqhd,brnkhd->brnhqk', qb, kc).astype(jnp.float32)
    qi = jnp.arange(nb)[:, None, None] * blk + jnp.arange(blk)[None, :, None]
    kj = (jnp.arange(nb)[:, None, None] - 1) * blk + jnp.arange(2 * blk)[None, None, :]
    rel = qi - kj
    mask = (rel >= 0) & (rel <= blk) & (kj >= 0)
    s = jnp.where(mask[None, None, :, None], s, -jnp.inf)
    m = jnp.max(s, axis=-1)
    p = jnp.exp(s - m[..., None])
    l = jnp.sum(p, axis=-1)
    num = jnp.einsum('brnhqk,brnkhd->brnqhd', p, vc.astype(jnp.float32))

    def stat_back(t):
        t = t.transpose(0, 1, 2, 4, 3).reshape(B, dil, lp, H)[:, :, :ls]
        return t.transpose(0, 2, 1, 3).reshape(B, S, H)

    num = num.reshape(B, dil, lp, H, Dh)[:, :, :ls].transpose(0, 2, 1, 3, 4).reshape(B, S, H, Dh)
    return stat_back(m), stat_back(l), num


def _dilated_attention(q, k, v):
    stats = [_dilated_branch(q, k, v, w, d) for (w, d) in A_PATTERNS]
    m_all = jnp.stack([st[0] for st in stats])
    l_all = jnp.stack([st[1] for st in stats])
    n_all = jnp.stack([st[2] for st in stats])
    w = jnp.exp(m_all - jnp.max(m_all, axis=0))
    return jnp.sum(w[..., None] * n_all, axis=0) / jnp.sum(w * l_all, axis=0)[..., None]


def _diff_attention(q, k, v, lam):
    B, S = q.shape[:2]
    nq = S // Q_BLOCK
    qblocks = q.reshape(B, nq, Q_BLOCK, B_HEADS, 2, HEAD_DIM).swapaxes(0, 1)
    kpos = jnp.arange(S)
    vf = v.astype(jnp.float32)

    def block(args):
        qb, n = args
        s = jnp.einsum('bqhcd,bkhcd->bhcqk', qb, k).astype(jnp.float32)
        qpos = n * Q_BLOCK + jnp.arange(Q_BLOCK)
        mask = kpos[None, :] <= qpos[:, None]
        a = jax.nn.softmax(jnp.where(mask, s, -jnp.inf), axis=-1)
        diff = a[:, :, 0] - lam * a[:, :, 1]
        return jnp.einsum('bhqk,bkhd->bqhd', diff, vf)

    out = lax.map(block, (qblocks, jnp.arange(nq)))
    return out.swapaxes(0, 1).reshape(B, S, B_HEADS, B_VDIM)


def _dsa_attention(q, k, v, qi, ki, wi):
    B, S = q.shape[:2]
    nq = S // Q_BLOCK
    topk = min(TOPK_MAX, S // 4)
    grp = C_HEADS // C_KV_HEADS
    kpos = jnp.arange(S)
    kif = ki.astype(jnp.float32)
    vf = v.astype(jnp.float32)

    def blocks(t):
        return t.reshape(B, nq, Q_BLOCK, *t.shape[2:]).swapaxes(0, 1)

    def block(args):
        qb, qib, wib, n = args
        qpos = n * Q_BLOCK + jnp.arange(Q_BLOCK)
        causal = kpos[None, :] <= qpos[:, None]
        rel = jax.nn.relu(jnp.einsum('bqhd,bkd->bqhk', qib.astype(jnp.float32), kif))
        score = jnp.einsum('bqhk,bqh->bqk', rel, wib.astype(jnp.float32))
        score = jnp.where(causal[None], score, -jnp.inf)
        _, idx = lax.top_k(score, topk)
        valid = idx <= qpos[None, :, None]
        k_sel = jax.vmap(lambda kb_, ib_: kb_[ib_])(k, idx)
        v_sel = jax.vmap(lambda vb_, ib_: vb_[ib_])(vf, idx)
        qg = qb.reshape(B, Q_BLOCK, C_KV_HEADS, grp, HEAD_DIM)
        s = jnp.einsum('bqgjd,bqkgd->bqgjk', qg, k_sel).astype(jnp.float32)
        s = jnp.where(valid[:, :, None, None, :], s, -jnp.inf)
        p = jax.nn.softmax(s, axis=-1)
        o = jnp.einsum('bqgjk,bqkgd->bqgjd', p, v_sel)
        return o.reshape(B, Q_BLOCK, C_QW)

    out = lax.map(block, (blocks(q), blocks(qi), blocks(wi), jnp.arange(nq)))
    return out.swapaxes(0, 1).reshape(B, S, C_QW)


def setup_inputs(seed: int = 0) -> dict:
    key = jax.random.key(seed)
    ks = jax.random.split(key, 16)
    f32 = jnp.float32
    nrm = lambda k, shape, s: jax.random.normal(k, shape, f32) * s
    return {
        "x": nrm(ks[0], (BATCH, SEQ, D_MODEL), 1.0),
        "norm_mix": 1.0 + nrm(ks[1], (DEPTH, D_MODEL), 0.02),
        "norm_ffn": 1.0 + nrm(ks[2], (DEPTH, D_MODEL), 0.02),
        "w_in_even": nrm(ks[3], (N_EVEN, D_MODEL, EVEN_IN), D_MODEL ** -0.5),
        "w_out_even": nrm(ks[4], (N_EVEN, A_W + B_W, D_MODEL), (A_W + B_W) ** -0.5),
        "lambda_q1": nrm(ks[5], (N_EVEN, HEAD_DIM), 0.1),
        "lambda_k1": nrm(ks[6], (N_EVEN, HEAD_DIM), 0.1),
        "lambda_q2": nrm(ks[7], (N_EVEN, HEAD_DIM), 0.1),
        "lambda_k2": nrm(ks[8], (N_EVEN, HEAD_DIM), 0.1),
        "diff_subln": 1.0 + nrm(ks[9], (N_EVEN, B_VDIM), 0.02),
        "w_in_odd": nrm(ks[10], (N_ODD, D_MODEL, ODD_IN), D_MODEL ** -0.5),
        "w_out_odd": nrm(ks[11], (N_ODD, C_QW, D_MODEL), C_QW ** -0.5),
        "w_ffn_up": nrm(ks[12], (DEPTH, D_MODEL, D_FF), D_MODEL ** -0.5),
        "w_ffn_down": nrm(ks[13], (DEPTH, D_FF, D_MODEL), D_FF ** -0.5),
        "norm_final": 1.0 + nrm(ks[14], (D_MODEL,), 0.02),
    }


def reference(x, norm_mix, norm_ffn, w_in_even, w_out_even, lambda_q1, lambda_k1,
              lambda_q2, lambda_k2, diff_subln, w_in_odd, w_out_odd, w_ffn_up,
              w_ffn_down, norm_final):
    B, S, _ = x.shape
    cos, sin = _rope_tables(S)
    scale = HEAD_DIM ** -0.5
    for layer in range(DEPTH):
        h = _rmsnorm(x, norm_mix[layer])
        if layer % 2 == 0:
            e = layer // 2
            proj = h @ w_in_even[e]
            qa, ka, va, qb, kb, vb = jnp.split(proj, EVEN_SPLITS, axis=-1)
            qa = _partial_rope(qa.reshape(B, S, A_HEADS, HEAD_DIM), cos, sin) * scale
            ka = _partial_rope(ka.reshape(B, S, A_HEADS, HEAD_DIM), cos, sin)
            va = va.reshape(B, S, A_HEADS, HEAD_DIM)
            out_a = _dilated_attention(qa, ka, va).astype(x.dtype).reshape(B, S, A_W)
            qb = (_partial_rope(qb.reshape(B, S, 2 * B_HEADS, HEAD_DIM), cos, sin) * scale
                  ).reshape(B, S, B_HEADS, 2, HEAD_DIM)
            kb = _partial_rope(kb.reshape(B, S, 2 * B_HEADS, HEAD_DIM), cos, sin
                               ).reshape(B, S, B_HEADS, 2, HEAD_DIM)
            vb = vb.reshape(B, S, B_HEADS, B_VDIM)
            lam_init = 0.8 - 0.6 * math.exp(-0.3 * layer)
            lam = (jnp.exp(jnp.sum((lambda_q1[e] * lambda_k1[e]).astype(jnp.float32)))
                   - jnp.exp(jnp.sum((lambda_q2[e] * lambda_k2[e]).astype(jnp.float32)))
                   + lam_init)
            ob = _rmsnorm(_diff_attention(qb, kb, vb, lam), diff_subln[e]) * (1.0 - lam_init)
            out_b = ob.astype(x.dtype).reshape(B, S, B_W)
            mixed = jnp.concatenate([out_a, out_b], axis=-1) @ w_out_even[e]
        else:
            o = layer // 2
            proj = h @ w_in_odd[o]
            q, k, v, qi, ki, wi = jnp.split(proj, ODD_SPLITS, axis=-1)
            q = _partial_rope(q.reshape(B, S, C_HEADS, HEAD_DIM), cos, sin) * scale
            k = _partial_rope(k.reshape(B, S, C_KV_HEADS, HEAD_DIM), cos, sin)
            v = v.reshape(B, S, C_KV_HEADS, HEAD_DIM)
            qi = _partial_rope(qi.reshape(B, S, IDX_HEADS, IDX_DIM), cos, sin) * (IDX_DIM ** -0.5)
            ki = _partial_rope(ki.reshape(B, S, 1, IDX_DIM), cos, sin)[:, :, 0]
            wi = wi * (IDX_HEADS ** -0.5)
            mixed = _dsa_attention(q, k, v, qi, ki, wi).astype(x.dtype) @ w_out_odd[o]
        x = x + mixed
        h = _rmsnorm(x, norm_ffn[layer])
        x = x + jnp.square(jax.nn.relu(h @ w_ffn_up[layer])) @ w_ffn_down[layer]
    return _rmsnorm(x, norm_final)
```

```python
import functools
import math

import jax
import jax.numpy as jnp
from jax import lax
from jax.experimental import pallas as pl
from jax.experimental.pallas import tpu as pltpu

D_MODEL = 1024
HEAD_DIM = 64
ROT_DIM = HEAD_DIM // 4
ROPE_THETA = 500000.0
NORM_EPS = 1e-6

A_HEADS = 8
A_PATTERNS = ((128, 1), (512, 4), (2048, 16))
A_W = A_HEADS * HEAD_DIM
B_HEADS = 4
B_VDIM = 2 * HEAD_DIM
B_W = B_HEADS * B_VDIM
C_HEADS = 16
C_KV_HEADS = 4
C_GROUP = C_HEADS // C_KV_HEADS
IDX_HEADS = 8
IDX_DIM = 64
TOPK_MAX = 256
D_FF = 4 * D_MODEL
C_QW = C_HEADS * HEAD_DIM
C_KVW = C_KV_HEADS * HEAD_DIM

LANES = 128
DIL_BLK = 128
NEG = -1e30
VMEM_LIMIT = 56 * 1024 * 1024

BF16 = jnp.bfloat16
F32 = jnp.float32


def _cparams(sem):
    return pltpu.CompilerParams(dimension_semantics=sem, vmem_limit_bytes=VMEM_LIMIT)


def _lo_lanes():
    return lax.broadcasted_iota(jnp.int32, (1, LANES), 1) < HEAD_DIM


def _rep(t, width):
    n = width // LANES
    return t if n == 1 else jnp.concatenate([t] * n, axis=1)


def _dot_nt(a, b):
    return lax.dot_general(a, b, (((1,), (1,)), ((), ())), preferred_element_type=F32)


def _rms(x, g):
    return x * lax.rsqrt(jnp.mean(x * x, axis=-1, keepdims=True) + NORM_EPS) * g


def _softmax_step(s, m_prev, l_prev):
    m_new = jnp.maximum(m_prev, jnp.max(s, axis=-1, keepdims=True))
    alpha = jnp.exp(m_prev - m_new)
    p = jnp.exp(s - _rep(m_new, s.shape[1]))
    l_new = alpha * l_prev + jnp.sum(p, axis=-1, keepdims=True)
    return p, m_new, l_new, alpha


def _rope_tables(seq_len):
    pos = jnp.arange(seq_len, dtype=F32)
    inv_freq = jnp.power(ROPE_THETA, -jnp.arange(0, ROT_DIM, 2, dtype=F32) / ROT_DIM)
    ang = pos[:, None] * inv_freq[None, :]
    cos, sin = jnp.cos(ang), jnp.sin(ang)
    half = ROT_DIM // 2
    rest = HEAD_DIM - ROT_DIM
    one = jnp.ones((seq_len, rest), F32)
    z_h = jnp.zeros((seq_len, half), F32)
    z_r = jnp.zeros((seq_len, rest), F32)
    c = jnp.concatenate([cos, cos, one], axis=1)
    sa = jnp.concatenate([z_h, sin, z_r], axis=1)
    sb = jnp.concatenate([-sin, z_h, z_r], axis=1)
    two = lambda t: jnp.concatenate([t, t], axis=1)
    return two(c), two(sa), two(sb)


def _proj_kernel(x_ref, g_ref, w_ref, c_ref, sa_ref, sb_ref, o_ref, *, chunks):
    h = _rms(x_ref[...], g_ref[...]).astype(BF16)
    c, sa, sb = c_ref[...], sa_ref[...], sb_ref[...]
    half = ROT_DIM // 2
    for start, width, rope, scale in chunks:
        acc = jnp.dot(h, w_ref[:, start:start + width], preferred_element_type=F32)
        if rope:
            acc = (acc * _rep(c, width)
                   + pltpu.roll(acc, half, 1) * _rep(sa, width)
                   + pltpu.roll(acc, width - half, 1) * _rep(sb, width))
        if scale != 1.0:
            acc = acc * scale
        o_ref[:, start:start + width] = acc.astype(o_ref.dtype)


def _norm_proj(x2, g, w, tables, chunks, seq_len, tm=256):
    m_rows, d = x2.shape
    n = w.shape[1]
    nt = seq_len // tm
    tab_spec = pl.BlockSpec((tm, LANES), lambda i: (i % nt, 0))
    return pl.pallas_call(
        functools.partial(_proj_kernel, chunks=chunks),
        out_shape=jax.ShapeDtypeStruct((m_rows, n), BF16),
        grid=(m_rows // tm,),
        in_specs=[pl.BlockSpec((tm, d), lambda i: (i, 0)),
                  pl.BlockSpec((1, d), lambda i: (0, 0)),
                  pl.BlockSpec((d, n), lambda i: (0, 0)),
                  tab_spec, tab_spec, tab_spec],
        out_specs=pl.BlockSpec((tm, n), lambda i: (i, 0)),
        compiler_params=_cparams(("parallel",)),
        name="norm_proj",
    )(x2, g.reshape(1, d), w, *tables)


def _dil_kernel(q_ref, kc_ref, kp_ref, vc_ref, vp_ref, o_ref, lse_ref):
    n = pl.program_id(2)
    blk = DIL_BLK
    row = lax.broadcasted_iota(jnp.int32, (blk, 2 * blk), 0)
    col = lax.broadcasted_iota(jnp.int32, (blk, 2 * blk), 1)
    rel = row - col + blk
    mask = (rel >= 0) & (rel <= blk) & ((col >= blk) | (n > 0))
    lo = _lo_lanes()
    for pair in range(A_W // LANES):
        sl = slice(pair * LANES, (pair + 1) * LANES)
        q = q_ref[0, :, sl]
        kcat = jnp.concatenate([kp_ref[0, :, sl], kc_ref[0, :, sl]], axis=0)
        vcat = jnp.concatenate([vp_ref[0, :, sl], vc_ref[0, :, sl]], axis=0)
        res = []
        for qm in (jnp.where(lo, q, 0), jnp.where(lo, 0, q)):
            s = jnp.where(mask, _dot_nt(qm, kcat), -jnp.inf)
            m = jnp.max(s, axis=-1, keepdims=True)
            p = jnp.exp(s - m)
            l = jnp.sum(p, axis=-1, keepdims=True)
            pv = jnp.dot(p.astype(BF16), vcat, preferred_element_type=F32)
            res.append((pv / l, m + jnp.log(l)))
        o_ref[0, :, sl] = jnp.where(lo, res[0][0], res[1][0])
        lse_ref[0, :, sl] = jnp.where(lo, res[0][1], res[1][1])


def _dilated_pattern(proj, dil, n_cols):
    b, s, _ = proj.shape
    ls = s // dil
    nb = ls // DIL_BLK
    cb = n_cols // A_W
    view = proj.reshape(b, ls, dil * n_cols)
    blk = (1, DIL_BLK, A_W)
    cur = lambda c: pl.BlockSpec(blk, lambda bi, r, n: (bi, n, r * cb + c))
    prev = lambda c: pl.BlockSpec(blk, lambda bi, r, n: (bi, jnp.maximum(n - 1, 0), r * cb + c))
    out_spec = pl.BlockSpec(blk, lambda bi, r, n: (bi, n, r))
    out_sds = jax.ShapeDtypeStruct((b, ls, dil * A_W), F32)
    o, lse = pl.pallas_call(
        _dil_kernel,
        out_shape=(out_sds, out_sds),
        grid=(b, dil, nb),
        in_specs=[cur(0), cur(1), prev(1), cur(2), prev(2)],
        out_specs=(out_spec, out_spec),
        compiler_params=_cparams(("parallel", "parallel", "arbitrary")),
        name=f"dilated_d{dil}",
    )(view, view, view, view, view)
    return o.reshape(b, s, A_W), lse.reshape(b, s, A_W)


def _merge_kernel(o1, o2, o3, l1, l2, l3, out_ref):
    a, b, c = l1[...], l2[...], l3[...]
    m = jnp.maximum(jnp.maximum(a, b), c)
    ea, eb, ec = jnp.exp(a - m), jnp.exp(b - m), jnp.exp(c - m)
    num = ea * o1[...] + eb * o2[...] + ec * o3[...]
    out_ref[...] = (num / (ea + eb + ec)).astype(out_ref.dtype)


def _dilated_merge(outs, lses, tm=512):
    b, s, w = outs[0].shape
    m_rows = b * s
    spec = pl.BlockSpec((tm, w), lambda i: (i, 0))
    flat = [t.reshape(m_rows, w) for t in (*outs, *lses)]
    return pl.pallas_call(
        _merge_kernel,
        out_shape=jax.ShapeDtypeStruct((m_rows, w), BF16),
        grid=(m_rows // tm,),
        in_specs=[spec] * 6,
        out_specs=spec,
        compiler_params=_cparams(("parallel",)),
        name="dilated_merge",
    )(*flat)


def _diff_kernel(q_ref, k_ref, v_ref, lq1, lk1, lq2, lk2, g_ref, o_ref,
                 m_sc, l_sc, acc_sc, *, tq, lam_init):
    i = pl.program_id(2)
    lo = _lo_lanes()
    q = q_ref[0]
    qms = (jnp.where(lo, q, 0), jnp.where(lo, 0, q))
    m_sc[...] = jnp.full(m_sc.shape, NEG, F32)
    l_sc[...] = jnp.zeros(l_sc.shape, F32)
    acc_sc[...] = jnp.zeros(acc_sc.shape, F32)

    def step(j, masked):
        start = pl.multiple_of(j * tq, tq)
        k = k_ref[0, pl.ds(start, tq), :]
        v = v_ref[0, pl.ds(start, tq), :]
        for c in range(2):
            s = _dot_nt(qms[c], k)
            if masked:
                row = lax.broadcasted_iota(jnp.int32, (tq, tq), 0)
                col = lax.broadcasted_iota(jnp.int32, (tq, tq), 1)
                s = jnp.where(col <= row, s, NEG)
            p, m_new, l_new, alpha = _softmax_step(s, m_sc[c], l_sc[c])
            m_sc[c] = m_new
            l_sc[c] = l_new
            acc_sc[c] = alpha * acc_sc[c] + jnp.dot(p.astype(BF16), v, preferred_element_type=F32)

    def body(j, carry):
        step(j, False)
        return carry

    lax.fori_loop(0, i, body, 0)
    step(i, True)

    lam = (jnp.exp(jnp.sum(lq1[...] * lk1[...], axis=-1, keepdims=True))
           - jnp.exp(jnp.sum(lq2[...] * lk2[...], axis=-1, keepdims=True)) + lam_init)
    o = acc_sc[0] / l_sc[0] - lam * (acc_sc[1] / l_sc[1])
    o_ref[0] = (_rms(o, g_ref[...]) * (1.0 - lam_init)).astype(o_ref.dtype)


def _diff_attention(proj, lq1, lk1, lq2, lk2, subln, lam_init, tq=256):
    b, s, _ = proj.shape
    qb, kb, vb = 3 * A_W // LANES, (3 * A_W + B_W) // LANES, (3 * A_W + 2 * B_W) // LANES
    vec = pl.BlockSpec((1, HEAD_DIM), lambda bi, h, i: (0, 0))
    return pl.pallas_call(
        functools.partial(_diff_kernel, tq=tq, lam_init=lam_init),
        out_shape=jax.ShapeDtypeStruct((b, s, B_W), BF16),
        grid=(b, B_HEADS, s // tq),
        in_specs=[pl.BlockSpec((1, tq, LANES), lambda bi, h, i: (bi, i, qb + h)),
                  pl.BlockSpec((1, s, LANES), lambda bi, h, i: (bi, 0, kb + h)),
                  pl.BlockSpec((1, s, LANES), lambda bi, h, i: (bi, 0, vb + h)),
                  vec, vec, vec, vec,
                  pl.BlockSpec((1, B_VDIM), lambda bi, h, i: (0, 0))],
        out_specs=pl.BlockSpec((1, tq, LANES), lambda bi, h, i: (bi, i, h)),
        scratch_shapes=[pltpu.VMEM((2, tq, LANES), F32)] * 3,
        compiler_params=_cparams(("parallel", "parallel", "arbitrary")),
        name="diff_attention",
    )(proj, proj, proj, lq1.reshape(1, -1), lk1.reshape(1, -1), lq2.reshape(1, -1),
      lk2.reshape(1, -1), subln.reshape(1, -1))


ODD_Q = 0
ODD_K = C_QW
ODD_V = ODD_K + C_KV_HEADS * LANES
ODD_QI = ODD_V + C_KV_HEADS * LANES
ODD_KI = ODD_QI + IDX_HEADS * IDX_DIM
ODD_WI = ODD_KI + LANES
ODD_N = ODD_WI + LANES


def _index_kernel(qi_ref, ki_ref, wi_ref, bias_ref, sc, cut_sc, *, tq, tk, nkc, topk, idx_bits):
    i = pl.program_id(1)
    nv = (i * tq) // tk + 1
    lo = _lo_lanes()
    kf = float(topk)
    qpos = i * tq + lax.broadcasted_iota(jnp.int32, (tq, tk), 0)
    col = lax.broadcasted_iota(jnp.int32, (tq, tk), 1)
    wi = wi_ref[0].astype(F32)
    qi = qi_ref[0]
    qms = []
    for pair in range(IDX_HEADS * IDX_DIM // LANES):
        qp = qi[:, pair * LANES:(pair + 1) * LANES]
        qms += [jnp.where(lo, qp, 0), jnp.where(lo, 0, qp)]

    def score_body(j, carry):
        start = pl.multiple_of(j * tk, tk)
        k = ki_ref[0, pl.ds(start, tk), :]
        score = jnp.zeros((tq, tk), F32)
        for h in range(IDX_HEADS):
            score = score + jnp.maximum(_dot_nt(qms[h], k), 0.0) * wi[:, h:h + 1]
        sc[j] = jnp.where(start + col <= qpos, score, -jnp.inf)
        return carry

    lax.fori_loop(0, nv, score_body, 0)

    def count(pred):
        def body(j, acc):
            c = jnp.where(pred(sc[j], j * tk + col), 1.0, 0.0)
            for t in range(tk // LANES):
                acc = acc + c[:, t * LANES:(t + 1) * LANES]
            return acc
        acc = lax.fori_loop(0, nv, body, jnp.zeros((tq, LANES), F32))
        return jnp.sum(acc, axis=-1, keepdims=True)

    neg = count(lambda s, _: s >= 0.0) < kf

    def bit_body(t, cur):
        cand = cur | lax.shift_left(jnp.int32(1), 30 - t)
        cf = lax.bitcast_convert_type(cand, F32)
        thr = jnp.where(neg, -cf, cf)
        cnt = count(lambda s, _: s >= thr)
        ok = jnp.logical_xor(cnt >= kf, neg)
        return jnp.where(ok, cand, cur)

    cur = lax.fori_loop(0, 31, bit_body, jnp.zeros((tq, 1), jnp.int32))
    mag = lax.bitcast_convert_type(jnp.where(neg, cur + 1, cur), F32)
    thr = jnp.where(neg, -mag, mag)

    cut_sc[...] = jnp.full(cut_sc.shape, (1 << idx_bits) - 1, jnp.int32)
    n_ge = count(lambda s, _: s >= thr)

    @pl.when(jnp.max(n_ge) > kf)
    def _():
        need = kf - count(lambda s, _: s > thr)

        def idx_body(t, cur):
            cand = cur | lax.shift_left(jnp.int32(1), idx_bits - 1 - t)
            cnt = count(lambda s, kidx: (s == thr) & (kidx < cand))
            return jnp.where(cnt < need, cand, cur)

        cut = lax.fori_loop(0, idx_bits, idx_body, jnp.zeros((tq, 1), jnp.int32))
        cut_sc[...] = jnp.broadcast_to(cut, cut_sc.shape)

    cut = cut_sc[:, 0:1]
    for j in range(nkc):
        @pl.when(j < nv)
        def _():
            s = sc[j]
            kidx = j * tk + col
            sel = ((s > thr) | ((s == thr) & (kidx <= cut))) & (kidx <= qpos)
            bias_ref[0, j] = jnp.where(sel, 0.0, NEG).astype(bias_ref.dtype)

        @pl.when(j >= nv)
        def _():
            bias_ref[0, j] = jnp.full((tq, tk), NEG, bias_ref.dtype)


def _dsa_select(proj, topk, tq=128, tk=512):
    b, s, _ = proj.shape
    nkc = s // tk
    idx_bits = max(1, (s - 1).bit_length())
    return pl.pallas_call(
        functools.partial(_index_kernel, tq=tq, tk=tk, nkc=nkc, topk=topk, idx_bits=idx_bits),
        out_shape=jax.ShapeDtypeStruct((b, nkc, s, tk), BF16),
        grid=(b, s // tq),
        in_specs=[pl.BlockSpec((1, tq, IDX_HEADS * IDX_DIM), lambda bi, i: (bi, i, ODD_QI // (IDX_HEADS * IDX_DIM))),
                  pl.BlockSpec((1, s, LANES), lambda bi, i: (bi, 0, ODD_KI // LANES)),
                  pl.BlockSpec((1, tq, LANES), lambda bi, i: (bi, i, ODD_WI // LANES))],
        out_specs=pl.BlockSpec((1, nkc, tq, tk), lambda bi, i: (bi, 0, i, 0)),
        scratch_shapes=[pltpu.VMEM((nkc, tq, tk), F32), pltpu.VMEM((tq, LANES), jnp.int32)],
        compiler_params=_cparams(("parallel", "arbitrary")),
        name="dsa_select",
    )(proj, proj, proj)


def _dsa_kernel(q_ref, k_ref, v_ref, bias_ref, o_ref, m_sc, l_sc, acc_sc, *, tq, tk):
    i = pl.program_id(1)
    nv = (i * tq) // tk + 1
    lo = _lo_lanes()
    n_pairs = C_QW // LANES
    pairs_per_group = C_GROUP * HEAD_DIM // LANES
    q = q_ref[0]
    qms = []
    for pair in range(n_pairs):
        qp = q[:, pair * LANES:(pair + 1) * LANES]
        qms.append((jnp.where(lo, qp, 0), jnp.where(lo, 0, qp)))
    m_sc[...] = jnp.full(m_sc.shape, NEG, F32)
    l_sc[...] = jnp.zeros(l_sc.shape, F32)
    acc_sc[...] = jnp.zeros(acc_sc.shape, F32)

    def body(j, carry):
        start = pl.multiple_of(j * tk, tk)
        bias = bias_ref[0, j].astype(F32)
        for pair in range(n_pairs):
            g = pair // pairs_per_group
            k = k_ref[0, pl.ds(start, tk), g * LANES:(g + 1) * LANES]
            v = v_ref[0, pl.ds(start, tk), g * LANES:(g + 1) * LANES]
            pv, al = [], []
            for c in range(2):
                h = 2 * pair + c
                s = _dot_nt(qms[pair][c], k) + bias
                p, m_new, l_new, alpha = _softmax_step(s, m_sc[h], l_sc[h])
                m_sc[h] = m_new
                l_sc[h] = l_new
                pv.append(jnp.dot(p.astype(BF16), v, preferred_element_type=F32))
                al.append(alpha)
            acc_sc[pair] = (jnp.where(lo, al[0], al[1]) * acc_sc[pair]
                            + jnp.where(lo, pv[0], pv[1]))
        return carry

    lax.fori_loop(0, nv, body, 0)
    for pair in range(n_pairs):
        l = jnp.where(lo, l_sc[2 * pair], l_sc[2 * pair + 1])
        o_ref[0, :, pair * LANES:(pair + 1) * LANES] = (acc_sc[pair] / l).astype(o_ref.dtype)


def _dsa_attention(proj, bias, tq=128, tk=512):
    b, s, _ = proj.shape
    nkc = s // tk
    kvw = C_KV_HEADS * LANES
    return pl.pallas_call(
        functools.partial(_dsa_kernel, tq=tq, tk=tk),
        out_shape=jax.ShapeDtypeStruct((b, s, C_QW), BF16),
        grid=(b, s // tq),
        in_specs=[pl.BlockSpec((1, tq, C_QW), lambda bi, i: (bi, i, ODD_Q // C_QW)),
                  pl.BlockSpec((1, s, kvw), lambda bi, i: (bi, 0, ODD_K // kvw)),
                  pl.BlockSpec((1, s, kvw), lambda bi, i: (bi, 0, ODD_V // kvw)),
                  pl.BlockSpec((1, nkc, tq, tk), lambda bi, i: (bi, 0, i, 0))],
        out_specs=pl.BlockSpec((1, tq, C_QW), lambda bi, i: (bi, i, 0)),
        scratch_shapes=[pltpu.VMEM((C_HEADS, tq, LANES), F32),
                        pltpu.VMEM((C_HEADS, tq, LANES), F32),
                        pltpu.VMEM((C_QW // LANES, tq, LANES), F32)],
        compiler_params=_cparams(("parallel", "arbitrary")),
        name="dsa_attention",
    )(proj, proj, proj, bias)


def _out_proj_kernel(*refs):
    x_ref, o_ref = refs[0], refs[-1]
    pairs = refs[1:-1]
    acc = x_ref[...]
    for a_ref, w_ref in zip(pairs[0::2], pairs[1::2]):
        acc = acc + jnp.dot(a_ref[...], w_ref[...], preferred_element_type=F32)
    o_ref[...] = acc


def _out_proj(x2, parts, tm=512):
    m_rows, d = x2.shape
    in_specs = [pl.BlockSpec((tm, d), lambda i: (i, 0))]
    args = [x2]
    for a, w in parts:
        in_specs += [pl.BlockSpec((tm, a.shape[1]), lambda i: (i, 0)),
                     pl.BlockSpec(w.shape, lambda i: (0, 0))]
        args += [a, w]
    return pl.pallas_call(
        _out_proj_kernel,
        out_shape=jax.ShapeDtypeStruct((m_rows, d), F32),
        grid=(m_rows // tm,),
        in_specs=in_specs,
        out_specs=pl.BlockSpec((tm, d), lambda i: (i, 0)),
        compiler_params=_cparams(("parallel",)),
        name="out_proj",
    )(*args)


def _ffn_kernel(x_ref, g_ref, wu_ref, wd_ref, gf_ref, o_ref, h_sc, acc_sc, *, final_norm):
    f = pl.program_id(1)

    @pl.when(f == 0)
    def _():
        h_sc[...] = _rms(x_ref[...], g_ref[...]).astype(BF16)
        acc_sc[...] = jnp.zeros(acc_sc.shape, F32)

    u = jnp.maximum(jnp.dot(h_sc[...], wu_ref[...], preferred_element_type=F32), 0.0)
    acc_sc[...] += jnp.dot((u * u).astype(BF16), wd_ref[...], preferred_element_type=F32)

    @pl.when(f == pl.num_programs(1) - 1)
    def _():
        y = x_ref[...] + acc_sc[...]
        o_ref[...] = _rms(y, gf_ref[...]) if final_norm else y


def _ffn(x2, g, wu, wd, g_final, final_norm, tm=1024, tf=512):
    m_rows, d = x2.shape
    tm = min(tm, m_rows)
    dff = wu.shape[1]
    return pl.pallas_call(
        functools.partial(_ffn_kernel, final_norm=final_norm),
        out_shape=jax.ShapeDtypeStruct((m_rows, d), F32),
        grid=(m_rows // tm, dff // tf),
        in_specs=[pl.BlockSpec((tm, d), lambda i, f: (i, 0)),
                  pl.BlockSpec((1, d), lambda i, f: (0, 0)),
                  pl.BlockSpec((d, tf), lambda i, f: (0, f)),
                  pl.BlockSpec((tf, d), lambda i, f: (f, 0)),
                  pl.BlockSpec((1, d), lambda i, f: (0, 0))],
        out_specs=pl.BlockSpec((tm, d), lambda i, f: (i, 0)),
        scratch_shapes=[pltpu.VMEM((tm, d), BF16), pltpu.VMEM((tm, d), F32)],
        compiler_params=_cparams(("parallel", "arbitrary")),
        name="ffn",
    )(x2, g.reshape(1, d), wu, wd, g_final.reshape(1, d))


def _even_chunks():
    scale = HEAD_DIM ** -0.5
    w = A_W
    return ((0, w, True, scale), (w, w, True, 1.0), (2 * w, w, False, 1.0),
            (3 * w, w, True, scale), (4 * w, w, True, 1.0), (5 * w, w, False, 1.0))


def _odd_chunks():
    w = 512
    return ((ODD_Q, w, True, HEAD_DIM ** -0.5), (ODD_Q + w, w, True, HEAD_DIM ** -0.5),
            (ODD_K, w, True, 1.0), (ODD_V, w, False, 1.0),
            (ODD_QI, w, True, IDX_DIM ** -0.5), (ODD_KI, LANES, True, 1.0),
            (ODD_WI, LANES, False, IDX_HEADS ** -0.5))


def _odd_weight(w):
    d = w.shape[0]
    q = w[:, :C_QW]
    k = w[:, C_QW:C_QW + C_KVW].reshape(d, C_KV_HEADS, 1, HEAD_DIM)
    v = w[:, C_QW + C_KVW:C_QW + 2 * C_KVW].reshape(d, C_KV_HEADS, 1, HEAD_DIM)
    dup = lambda t: jnp.broadcast_to(t, (d, C_KV_HEADS, 2, HEAD_DIM)).reshape(d, C_KV_HEADS * LANES)
    o = C_QW + 2 * C_KVW
    qi = w[:, o:o + IDX_HEADS * IDX_DIM]
    ki = w[:, o + IDX_HEADS * IDX_DIM:o + IDX_HEADS * IDX_DIM + IDX_DIM]
    wi = w[:, o + IDX_HEADS * IDX_DIM + IDX_DIM:]
    pad = jnp.zeros((d, LANES - IDX_HEADS), w.dtype)
    return jnp.concatenate([q, dup(k), dup(v), qi, ki, ki, wi, pad], axis=1)


def kernel(x, norm_mix, norm_ffn, w_in_even, w_out_even, lambda_q1, lambda_k1, lambda_q2,
           lambda_k2, diff_subln, w_in_odd, w_out_odd, w_ffn_up, w_ffn_down, norm_final):
    b, s, d = x.shape
    depth = norm_mix.shape[0]
    tables = _rope_tables(s)
    topk = min(TOPK_MAX, s // 4)
    x2 = x.reshape(b * s, d)
    for layer in range(depth):
        if layer % 2 == 0:
            e = layer // 2
            proj = _norm_proj(x2, norm_mix[layer], w_in_even[e].astype(BF16), tables,
                              _even_chunks(), s)
            n_cols = proj.shape[1]
            proj3 = proj.reshape(b, s, n_cols)
            stats = [_dilated_pattern(proj3, dil, n_cols) for (_, dil) in A_PATTERNS]
            out_a = _dilated_merge([st[0] for st in stats], [st[1] for st in stats])
            lam_init = 0.8 - 0.6 * math.exp(-0.3 * layer)
            out_b = _diff_attention(proj3, lambda_q1[e], lambda_k1[e], lambda_q2[e], lambda_k2[e],
                                    diff_subln[e], lam_init)
            wo = w_out_even[e].astype(BF16)
            x2 = _out_proj(x2, [(out_a, wo[:A_W]), (out_b.reshape(b * s, B_W), wo[A_W:])])
        else:
            o = layer // 2
            proj = _norm_proj(x2, norm_mix[layer], _odd_weight(w_in_odd[o]).astype(BF16), tables,
                              _odd_chunks(), s)
            proj3 = proj.reshape(b, s, proj.shape[1])
            bias = _dsa_select(proj3, topk)
            out_c = _dsa_attention(proj3, bias)
            x2 = _out_proj(x2, [(out_c.reshape(b * s, C_QW), w_out_odd[o].astype(BF16))])
        x2 = _ffn(x2, norm_ffn[layer], w_ffn_up[layer].astype(BF16), w_ffn_down[layer].astype(BF16),
                  norm_final, layer == depth - 1)
    return x2.reshape(b, s, d)
```

```python
import functools
import math

import jax
import jax.numpy as jnp
from jax import lax
from jax.experimental import pallas as pl
from jax.experimental.pallas import tpu as pltpu

D_MODEL = 1024
HEAD_DIM = 64
ROT_DIM = HEAD_DIM // 4
ROPE_THETA = 500000.0
NORM_EPS = 1e-6

A_HEADS = 8
A_PATTERNS = ((128, 1), (512, 4), (2048, 16))
A_W = A_HEADS * HEAD_DIM
B_HEADS = 4
B_VDIM = 2 * HEAD_DIM
B_W = B_HEADS * B_VDIM
C_HEADS = 16
C_KV_HEADS = 4
C_GROUP = C_HEADS // C_KV_HEADS
IDX_HEADS = 8
IDX_DIM = 64
TOPK_MAX = 256
D_FF = 4 * D_MODEL
C_QW = C_HEADS * HEAD_DIM
C_KVW = C_KV_HEADS * HEAD_DIM

LANES = 128
DIL_BLK = 128
NEG = -1e30
VMEM_LIMIT = 56 * 1024 * 1024

BF16 = jnp.bfloat16
F32 = jnp.float32


def _cparams(sem):
    return pltpu.CompilerParams(dimension_semantics=sem, vmem_limit_bytes=VMEM_LIMIT)


def _lo_lanes():
    return lax.broadcasted_iota(jnp.int32, (1, LANES), 1) < HEAD_DIM


def _rep(t, width):
    n = width // LANES
    return t if n == 1 else jnp.concatenate([t] * n, axis=1)


def _dot_nt(a, b):
    return lax.dot_general(a, b, (((1,), (1,)), ((), ())), preferred_element_type=F32)


def _rms(x, g):
    return x * lax.rsqrt(jnp.mean(x * x, axis=-1, keepdims=True) + NORM_EPS) * g


def _softmax_step(s, m_prev, l_prev):
    m_new = jnp.maximum(m_prev, jnp.max(s, axis=-1, keepdims=True))
    alpha = jnp.exp(m_prev - m_new)
    p = jnp.exp(s - _rep(m_new, s.shape[1]))
    l_new = alpha * l_prev + jnp.sum(p, axis=-1, keepdims=True)
    return p, m_new, l_new, alpha


def _rope_tables(seq_len):
    pos = jnp.arange(seq_len, dtype=F32)
    inv_freq = jnp.power(ROPE_THETA, -jnp.arange(0, ROT_DIM, 2, dtype=F32) / ROT_DIM)
    ang = pos[:, None] * inv_freq[None, :]
    cos, sin = jnp.cos(ang), jnp.sin(ang)
    half = ROT_DIM // 2
    rest = HEAD_DIM - ROT_DIM
    one = jnp.ones((seq_len, rest), F32)
    z_h = jnp.zeros((seq_len, half), F32)
    z_r = jnp.zeros((seq_len, rest), F32)
    c = jnp.concatenate([cos, cos, one], axis=1)
    sa = jnp.concatenate([z_h, sin, z_r], axis=1)
    sb = jnp.concatenate([-sin, z_h, z_r], axis=1)
    two = lambda t: jnp.concatenate([t, t], axis=1)
    return two(c), two(sa), two(sb)


def _proj_kernel(x_ref, g_ref, w_ref, c_ref, sa_ref, sb_ref, o_ref, *, chunks):
    h = _rms(x_ref[...], g_ref[...]).astype(BF16)
    c, sa, sb = c_ref[...], sa_ref[...], sb_ref[...]
    half = ROT_DIM // 2
    for start, width, rope, scale in chunks:
        acc = jnp.dot(h, w_ref[:, start:start + width], preferred_element_type=F32)
        if rope:
            acc = (acc * _rep(c, width)
                   + pltpu.roll(acc, half, 1) * _rep(sa, width)
                   + pltpu.roll(acc, width - half, 1) * _rep(sb, width))
        if scale != 1.0:
            acc = acc * scale
        o_ref[:, start:start + width] = acc.astype(o_ref.dtype)


def _norm_proj(x2, g, w, tables, chunks, seq_len, tm=256):
    m_rows, d = x2.shape
    n = w.shape[1]
    nt = seq_len // tm
    tab_spec = pl.BlockSpec((tm, LANES), lambda i: (i % nt, 0))
    return pl.pallas_call(
        functools.partial(_proj_kernel, chunks=chunks),
        out_shape=jax.ShapeDtypeStruct((m_rows, n), BF16),
        grid=(m_rows // tm,),
        in_specs=[pl.BlockSpec((tm, d), lambda i: (i, 0)),
                  pl.BlockSpec((1, d), lambda i: (0, 0)),
                  pl.BlockSpec((d, n), lambda i: (0, 0)),
                  tab_spec, tab_spec, tab_spec],
        out_specs=pl.BlockSpec((tm, n), lambda i: (i, 0)),
        compiler_params=_cparams(("parallel",)),
        name="norm_proj",
    )(x2, g.reshape(1, d), w, *tables)


def _dil_kernel(q_ref, kc_ref, kp_ref, vc_ref, vp_ref, o_ref, lse_ref):
    n = pl.program_id(2)
    blk = DIL_BLK
    row = lax.broadcasted_iota(jnp.int32, (blk, 2 * blk), 0)
    col = lax.broadcasted_iota(jnp.int32, (blk, 2 * blk), 1)
    rel = row - col + blk
    mask = (rel >= 0) & (rel <= blk) & ((col >= blk) | (n > 0))
    lo = _lo_lanes()
    for pair in range(A_W // LANES):
        sl = slice(pair * LANES, (pair + 1) * LANES)
        q = q_ref[0, :, sl]
        kcat = jnp.concatenate([kp_ref[0, :, sl], kc_ref[0, :, sl]], axis=0)
        vcat = jnp.concatenate([vp_ref[0, :, sl], vc_ref[0, :, sl]], axis=0)
        res = []
        for qm in (jnp.where(lo, q, 0), jnp.where(lo, 0, q)):
            s = jnp.where(mask, _dot_nt(qm, kcat), -jnp.inf)
            m = jnp.max(s, axis=-1, keepdims=True)
            p = jnp.exp(s - m)
            l = jnp.sum(p, axis=-1, keepdims=True)
            pv = jnp.dot(p.astype(BF16), vcat, preferred_element_type=F32)
            res.append((pv / l, m + jnp.log(l)))
        o_ref[0, :, sl] = jnp.where(lo, res[0][0], res[1][0])
        lse_ref[0, :, sl] = jnp.where(lo, res[0][1], res[1][1])


def _dilated_pattern(proj, dil, n_cols):
    b, s, _ = proj.shape
    ls = s // dil
    nb = ls // DIL_BLK
    cb = n_cols // A_W
    view = proj.reshape(b, ls, dil * n_cols)
    blk = (1, DIL_BLK, A_W)
    cur = lambda c: pl.BlockSpec(blk, lambda bi, r, n: (bi, n, r * cb + c))
    prev = lambda c: pl.BlockSpec(blk, lambda bi, r, n: (bi, jnp.maximum(n - 1, 0), r * cb + c))
    out_spec = pl.BlockSpec(blk, lambda bi, r, n: (bi, n, r))
    out_sds = jax.ShapeDtypeStruct((b, ls, dil * A_W), F32)
    o, lse = pl.pallas_call(
        _dil_kernel,
        out_shape=(out_sds, out_sds),
        grid=(b, dil, nb),
        in_specs=[cur(0), cur(1), prev(1), cur(2), prev(2)],
        out_specs=(out_spec, out_spec),
        compiler_params=_cparams(("parallel", "parallel", "arbitrary")),
        name=f"dilated_d{dil}",
    )(view, view, view, view, view)
    return o.reshape(b, s, A_W), lse.reshape(b, s, A_W)


def _merge_kernel(o1, o2, o3, l1, l2, l3, out_ref):
    a, b, c = l1[...], l2[...], l3[...]
    m = jnp.maximum(jnp.maximum(a, b), c)
    ea, eb, ec = jnp.exp(a - m), jnp.exp(b - m), jnp.exp(c - m)
    num = ea * o1[...] + eb * o2[...] + ec * o3[...]
    out_ref[...] = (num / (ea + eb + ec)).astype(out_ref.dtype)


def _dilated_merge(outs, lses, tm=512):
    b, s, w = outs[0].shape
    m_rows = b * s
    spec = pl.BlockSpec((tm, w), lambda i: (i, 0))
    flat = [t.reshape(m_rows, w) for t in (*outs, *lses)]
    return pl.pallas_call(
        _merge_kernel,
        out_shape=jax.ShapeDtypeStruct((m_rows, w), BF16),
        grid=(m_rows // tm,),
        in_specs=[spec] * 6,
        out_specs=spec,
        compiler_params=_cparams(("parallel",)),
        name="dilated_merge",
    )(*flat)


def _diff_kernel(q_ref, k_ref, v_ref, lq1, lk1, lq2, lk2, g_ref, o_ref,
                 qs_sc, m_sc, l_sc, acc_sc, *, tq, lam_init):
    i = pl.program_id(2)
    lo = _lo_lanes()
    q = q_ref[0]
    qs_sc[...] = jnp.concatenate([jnp.where(lo, q, 0), jnp.where(lo, 0, q)], axis=0)
    m_sc[...] = jnp.full(m_sc.shape, NEG, F32)
    l_sc[...] = jnp.zeros(l_sc.shape, F32)
    acc_sc[...] = jnp.zeros(acc_sc.shape, F32)

    def step(j, masked):
        start = pl.multiple_of(j * tq, tq)
        k = k_ref[0, pl.ds(start, tq), :]
        v = v_ref[0, pl.ds(start, tq), :]
        s = _dot_nt(qs_sc[...], k)
        if masked:
            row = lax.broadcasted_iota(jnp.int32, (2 * tq, tq), 0)
            col = lax.broadcasted_iota(jnp.int32, (2 * tq, tq), 1)
            s = jnp.where(col <= jnp.where(row >= tq, row - tq, row), s, NEG)
        p, m_new, l_new, alpha = _softmax_step(s, m_sc[...], l_sc[...])
        m_sc[...] = m_new
        l_sc[...] = l_new
        acc_sc[...] = alpha * acc_sc[...] + jnp.dot(p.astype(BF16), v, preferred_element_type=F32)

    def body(j, carry):
        step(j, False)
        return carry

    lax.fori_loop(0, i, body, 0)
    step(i, True)

    lam = (jnp.exp(jnp.sum(lq1[...] * lk1[...], axis=-1, keepdims=True))
           - jnp.exp(jnp.sum(lq2[...] * lk2[...], axis=-1, keepdims=True)) + lam_init)
    o = acc_sc[...] / l_sc[...]
    o = o[:tq] - lam * o[tq:]
    o_ref[0] = (_rms(o, g_ref[...]) * (1.0 - lam_init)).astype(o_ref.dtype)


def _diff_attention(proj, lq1, lk1, lq2, lk2, subln, lam_init, tq=512):
    b, s, _ = proj.shape
    qb, kb, vb = 3 * A_W // LANES, (3 * A_W + B_W) // LANES, (3 * A_W + 2 * B_W) // LANES
    vec = pl.BlockSpec((1, HEAD_DIM), lambda bi, h, i: (0, 0))
    return pl.pallas_call(
        functools.partial(_diff_kernel, tq=tq, lam_init=lam_init),
        out_shape=jax.ShapeDtypeStruct((b, s, B_W), BF16),
        grid=(b, B_HEADS, s // tq),
        in_specs=[pl.BlockSpec((1, tq, LANES), lambda bi, h, i: (bi, i, qb + h)),
                  pl.BlockSpec((1, s, LANES), lambda bi, h, i: (bi, 0, kb + h)),
                  pl.BlockSpec((1, s, LANES), lambda bi, h, i: (bi, 0, vb + h)),
                  vec, vec, vec, vec,
                  pl.BlockSpec((1, B_VDIM), lambda bi, h, i: (0, 0))],
        out_specs=pl.BlockSpec((1, tq, LANES), lambda bi, h, i: (bi, i, h)),
        scratch_shapes=[pltpu.VMEM((2 * tq, LANES), BF16)] + [pltpu.VMEM((2 * tq, LANES), F32)] * 3,
        compiler_params=_cparams(("parallel", "parallel", "arbitrary")),
        name="diff_attention",
    )(proj, proj, proj, lq1.reshape(1, -1), lk1.reshape(1, -1), lq2.reshape(1, -1),
      lk2.reshape(1, -1), subln.reshape(1, -1))


ODD_Q = 0
ODD_K = C_QW
ODD_V = ODD_K + C_KV_HEADS * LANES
ODD_QI = ODD_V + C_KV_HEADS * LANES
ODD_KI = ODD_QI + IDX_HEADS * IDX_DIM
ODD_WI = ODD_KI + LANES
ODD_N = ODD_WI + LANES


def _stack_heads(x, lo):
    parts = []
    for pair in range(x.shape[1] // LANES):
        xp = x[:, pair * LANES:(pair + 1) * LANES]
        parts += [jnp.where(lo, xp, 0), jnp.where(lo, 0, xp)]
    return jnp.concatenate(parts, axis=0)


def _index_kernel(qi_ref, ki_ref, wi_ref, bias_ref, sc, qs_sc, cut_sc, *, tq, tk, nkc, topk, idx_bits):
    i = pl.program_id(1)
    nv = (i * tq) // tk + 1
    kf = float(topk)
    krow = lax.broadcasted_iota(jnp.int32, (tk, tq), 0)
    qpos = i * tq + lax.broadcasted_iota(jnp.int32, (tk, tq), 1)
    qs_sc[...] = _stack_heads(qi_ref[0], _lo_lanes())
    w_t = wi_ref[0].astype(F32).T

    def score_body(j, carry):
        start = pl.multiple_of(j * tk, tk)
        s = _dot_nt(ki_ref[0, pl.ds(start, tk), :], qs_sc[...])
        score = jnp.zeros((tk, tq), F32)
        for h in range(IDX_HEADS):
            score = score + jnp.maximum(s[:, h * tq:(h + 1) * tq], 0.0) * w_t[h:h + 1, :]
        sc[j] = jnp.where(start + krow <= qpos, score, -jnp.inf)
        return carry

    lax.fori_loop(0, nv, score_body, 0)

    def count(pred):
        def body(j, acc):
            c = jnp.where(pred(sc[j], j * tk + krow), 1.0, 0.0)
            return acc + jnp.sum(c.reshape(tk // 8, 8, tq), axis=0)
        acc = lax.fori_loop(0, nv, body, jnp.zeros((8, tq), F32))
        return jnp.sum(acc, axis=0, keepdims=True)

    neg = count(lambda s, _: s >= 0.0) < kf

    def bit_body(t, cur):
        cand = cur | lax.shift_left(jnp.int32(1), 30 - t)
        cf = lax.bitcast_convert_type(cand, F32)
        thr = jnp.where(neg, -cf, cf)
        cnt = count(lambda s, _: s >= thr)
        return jnp.where(jnp.logical_xor(cnt >= kf, neg), cand, cur)

    cur = lax.fori_loop(0, 31, bit_body, jnp.zeros((1, tq), jnp.int32))
    mag = lax.bitcast_convert_type(jnp.where(neg, cur + 1, cur), F32)
    thr = jnp.where(neg, -mag, mag)

    cut_sc[...] = jnp.full(cut_sc.shape, (1 << idx_bits) - 1, jnp.int32)
    n_ge = count(lambda s, _: s >= thr)

    @pl.when(jnp.max(n_ge) > kf)
    def _():
        need = kf - count(lambda s, _: s > thr)

        def idx_body(t, cur):
            cand = cur | lax.shift_left(jnp.int32(1), idx_bits - 1 - t)
            cnt = count(lambda s, kidx: (s == thr) & (kidx < cand))
            return jnp.where(cnt < need, cand, cur)

        cut = lax.fori_loop(0, idx_bits, idx_body, jnp.zeros((1, tq), jnp.int32))
        cut_sc[...] = jnp.broadcast_to(cut, cut_sc.shape)

    cut = cut_sc[0:1, :]
    for j in range(nkc):
        @pl.when(j < nv)
        def _():
            s = sc[j]
            kidx = j * tk + krow
            sel = ((s > thr) | ((s == thr) & (kidx <= cut))) & (kidx <= qpos)
            bias_ref[0, j] = jnp.where(sel, 0.0, NEG).T.astype(bias_ref.dtype)

        @pl.when(j >= nv)
        def _():
            bias_ref[0, j] = jnp.full((tq, tk), NEG, bias_ref.dtype)


def _dsa_select(proj, topk, tq=256, tk=512):
    b, s, _ = proj.shape
    nkc = s // tk
    idx_bits = max(1, (s - 1).bit_length())
    qiw = IDX_HEADS * IDX_DIM
    return pl.pallas_call(
        functools.partial(_index_kernel, tq=tq, tk=tk, nkc=nkc, topk=topk, idx_bits=idx_bits),
        out_shape=jax.ShapeDtypeStruct((b, nkc, s, tk), BF16),
        grid=(b, s // tq),
        in_specs=[pl.BlockSpec((1, tq, qiw), lambda bi, i: (bi, i, ODD_QI // qiw)),
                  pl.BlockSpec((1, s, LANES), lambda bi, i: (bi, 0, ODD_KI // LANES)),
                  pl.BlockSpec((1, tq, LANES), lambda bi, i: (bi, i, ODD_WI // LANES))],
        out_specs=pl.BlockSpec((1, nkc, tq, tk), lambda bi, i: (bi, 0, i, 0)),
        scratch_shapes=[pltpu.VMEM((nkc, tk, tq), F32),
                        pltpu.VMEM((IDX_HEADS * tq, LANES), BF16),
                        pltpu.VMEM((8, tq), jnp.int32)],
        compiler_params=_cparams(("parallel", "arbitrary")),
        name="dsa_select",
    )(proj, proj, proj)


def _dsa_kernel(q_ref, k_ref, v_ref, bias_ref, o_ref, qs_sc, m_sc, l_sc, acc_sc, *, tq, tk):
    i = pl.program_id(1)
    nv = (i * tq) // tk + 1
    lo = _lo_lanes()
    gw = C_GROUP * HEAD_DIM
    for g in range(C_KV_HEADS):
        qs_sc[g] = _stack_heads(q_ref[0, :, g * gw:(g + 1) * gw], lo)
    m_sc[...] = jnp.full(m_sc.shape, NEG, F32)
    l_sc[...] = jnp.zeros(l_sc.shape, F32)
    acc_sc[...] = jnp.zeros(acc_sc.shape, F32)

    def body(j, carry):
        start = pl.multiple_of(j * tk, tk)
        bias = bias_ref[0, j].astype(F32)
        bias = jnp.concatenate([bias] * C_GROUP, axis=0)
        for g in range(C_KV_HEADS):
            k = k_ref[0, pl.ds(start, tk), g * LANES:(g + 1) * LANES]
            v = v_ref[0, pl.ds(start, tk), g * LANES:(g + 1) * LANES]
            s = _dot_nt(qs_sc[g], k) + bias
            p, m_new, l_new, alpha = _softmax_step(s, m_sc[g], l_sc[g])
            m_sc[g] = m_new
            l_sc[g] = l_new
            acc_sc[g] = alpha * acc_sc[g] + jnp.dot(p.astype(BF16), v, preferred_element_type=F32)
        return carry

    lax.fori_loop(0, nv, body, 0)
    for g in range(C_KV_HEADS):
        o = acc_sc[g] / l_sc[g]
        for pair in range(gw // LANES):
            r = 2 * pair * tq
            col = g * gw + pair * LANES
            o_ref[0, :, col:col + LANES] = jnp.where(lo, o[r:r + tq], o[r + tq:r + 2 * tq]).astype(o_ref.dtype)


def _dsa_attention(proj, bias, tq=256, tk=512):
    b, s, _ = proj.shape
    nkc = s // tk
    kvw = C_KV_HEADS * LANES
    state = pltpu.VMEM((C_KV_HEADS, C_GROUP * tq, LANES), F32)
    return pl.pallas_call(
        functools.partial(_dsa_kernel, tq=tq, tk=tk),
        out_shape=jax.ShapeDtypeStruct((b, s, C_QW), BF16),
        grid=(b, s // tq),
        in_specs=[pl.BlockSpec((1, tq, C_QW), lambda bi, i: (bi, i, ODD_Q // C_QW)),
                  pl.BlockSpec((1, s, kvw), lambda bi, i: (bi, 0, ODD_K // kvw)),
                  pl.BlockSpec((1, s, kvw), lambda bi, i: (bi, 0, ODD_V // kvw)),
                  pl.BlockSpec((1, nkc, tq, tk), lambda bi, i: (bi, 0, i, 0))],
        out_specs=pl.BlockSpec((1, tq, C_QW), lambda bi, i: (bi, i, 0)),
        scratch_shapes=[pltpu.VMEM((C_KV_HEADS, C_GROUP * tq, LANES), BF16), state, state, state],
        compiler_params=_cparams(("parallel", "arbitrary")),
        name="dsa_attention",
    )(proj, proj, proj, bias)


def _out_proj_kernel(*refs):
    x_ref, o_ref = refs[0], refs[-1]
    pairs = refs[1:-1]
    acc = x_ref[...]
    for a_ref, w_ref in zip(pairs[0::2], pairs[1::2]):
        acc = acc + jnp.dot(a_ref[...], w_ref[...], preferred_element_type=F32)
    o_ref[...] = acc


def _out_proj(x2, parts, tm=512):
    m_rows, d = x2.shape
    in_specs = [pl.BlockSpec((tm, d), lambda i: (i, 0))]
    args = [x2]
    for a, w in parts:
        in_specs += [pl.BlockSpec((tm, a.shape[1]), lambda i: (i, 0)),
                     pl.BlockSpec(w.shape, lambda i: (0, 0))]
        args += [a, w]
    return pl.pallas_call(
        _out_proj_kernel,
        out_shape=jax.ShapeDtypeStruct((m_rows, d), F32),
        grid=(m_rows // tm,),
        in_specs=in_specs,
        out_specs=pl.BlockSpec((tm, d), lambda i: (i, 0)),
        compiler_params=_cparams(("parallel",)),
        name="out_proj",
    )(*args)


def _ffn_kernel(x_ref, g_ref, wu_ref, wd_ref, gf_ref, o_ref, h_sc, acc_sc, *, final_norm):
    f = pl.program_id(1)

    @pl.when(f == 0)
    def _():
        h_sc[...] = _rms(x_ref[...], g_ref[...]).astype(BF16)
        acc_sc[...] = jnp.zeros(acc_sc.shape, F32)

    u = jnp.maximum(jnp.dot(h_sc[...], wu_ref[...], preferred_element_type=F32), 0.0)
    acc_sc[...] += jnp.dot((u * u).astype(BF16), wd_ref[...], preferred_element_type=F32)

    @pl.when(f == pl.num_programs(1) - 1)
    def _():
        y = x_ref[...] + acc_sc[...]
        o_ref[...] = _rms(y, gf_ref[...]) if final_norm else y


def _ffn(x2, g, wu, wd, g_final, final_norm, tm=1024, tf=512):
    m_rows, d = x2.shape
    tm = min(tm, m_rows)
    dff = wu.shape[1]
    return pl.pallas_call(
        functools.partial(_ffn_kernel, final_norm=final_norm),
        out_shape=jax.ShapeDtypeStruct((m_rows, d), F32),
        grid=(m_rows // tm, dff // tf),
        in_specs=[pl.BlockSpec((tm, d), lambda i, f: (i, 0)),
                  pl.BlockSpec((1, d), lambda i, f: (0, 0)),
                  pl.BlockSpec((d, tf), lambda i, f: (0, f)),
                  pl.BlockSpec((tf, d), lambda i, f: (f, 0)),
                  pl.BlockSpec((1, d), lambda i, f: (0, 0))],
        out_specs=pl.BlockSpec((tm, d), lambda i, f: (i, 0)),
        scratch_shapes=[pltpu.VMEM((tm, d), BF16), pltpu.VMEM((tm, d), F32)],
        compiler_params=_cparams(("parallel", "arbitrary")),
        name="ffn",
    )(x2, g.reshape(1, d), wu, wd, g_final.reshape(1, d))


def _even_chunks():
    scale = HEAD_DIM ** -0.5
    w = A_W
    return ((0, w, True, scale), (w, w, True, 1.0), (2 * w, w, False, 1.0),
            (3 * w, w, True, scale), (4 * w, w, True, 1.0), (5 * w, w, False, 1.0))


def _odd_chunks():
    w = 512
    return ((ODD_Q, w, True, HEAD_DIM ** -0.5), (ODD_Q + w, w, True, HEAD_DIM ** -0.5),
            (ODD_K, w, True, 1.0), (ODD_V, w, False, 1.0),
            (ODD_QI, w, True, IDX_DIM ** -0.5), (ODD_KI, LANES, True, 1.0),
            (ODD_WI, LANES, False, IDX_HEADS ** -0.5))


def _odd_weight(w):
    d = w.shape[0]
    q = w[:, :C_QW]
    k = w[:, C_QW:C_QW + C_KVW].reshape(d, C_KV_HEADS, 1, HEAD_DIM)
    v = w[:, C_QW + C_KVW:C_QW + 2 * C_KVW].reshape(d, C_KV_HEADS, 1, HEAD_DIM)
    dup = lambda t: jnp.broadcast_to(t, (d, C_KV_HEADS, 2, HEAD_DIM)).reshape(d, C_KV_HEADS * LANES)
    o = C_QW + 2 * C_KVW
    qi = w[:, o:o + IDX_HEADS * IDX_DIM]
    ki = w[:, o + IDX_HEADS * IDX_DIM:o + IDX_HEADS * IDX_DIM + IDX_DIM]
    wi = w[:, o + IDX_HEADS * IDX_DIM + IDX_DIM:]
    pad = jnp.zeros((d, LANES - IDX_HEADS), w.dtype)
    return jnp.concatenate([q, dup(k), dup(v), qi, ki, ki, wi, pad], axis=1)


def kernel(x, norm_mix, norm_ffn, w_in_even, w_out_even, lambda_q1, lambda_k1, lambda_q2,
           lambda_k2, diff_subln, w_in_odd, w_out_odd, w_ffn_up, w_ffn_down, norm_final):
    b, s, d = x.shape
    depth = norm_mix.shape[0]
    tables = _rope_tables(s)
    topk = min(TOPK_MAX, s // 4)
    x2 = x.reshape(b * s, d)
    for layer in range(depth):
        if layer % 2 == 0:
            e = layer // 2
            proj = _norm_proj(x2, norm_mix[layer], w_in_even[e].astype(BF16), tables,
                              _even_chunks(), s)
            n_cols = proj.shape[1]
            proj3 = proj.reshape(b, s, n_cols)
            stats = [_dilated_pattern(proj3, dil, n_cols) for (_, dil) in A_PATTERNS]
            out_a = _dilated_merge([st[0] for st in stats], [st[1] for st in stats])
            lam_init = 0.8 - 0.6 * math.exp(-0.3 * layer)
            out_b = _diff_attention(proj3, lambda_q1[e], lambda_k1[e], lambda_q2[e], lambda_k2[e],
                                    diff_subln[e], lam_init)
            wo = w_out_even[e].astype(BF16)
            x2 = _out_proj(x2, [(out_a, wo[:A_W]), (out_b.reshape(b * s, B_W), wo[A_W:])])
        else:
            o = layer // 2
            proj = _norm_proj(x2, norm_mix[layer], _odd_weight(w_in_odd[o]).astype(BF16), tables,
                              _odd_chunks(), s)
            proj3 = proj.reshape(b, s, proj.shape[1])
            bias = _dsa_select(proj3, topk)
            out_c = _dsa_attention(proj3, bias)
            x2 = _out_proj(x2, [(out_c.reshape(b * s, C_QW), w_out_odd[o].astype(BF16))])
        x2 = _ffn(x2, norm_ffn[layer], w_ffn_up[layer].astype(BF16), w_ffn_down[layer].astype(BF16),
                  norm_final, layer == depth - 1)
    return x2.reshape(b, s, d)
```

```python
import functools
import math

import jax
import jax.numpy as jnp
from jax import lax
from jax.experimental import pallas as pl
from jax.experimental.pallas import tpu as pltpu

D_MODEL = 1024
HEAD_DIM = 64
ROT_DIM = HEAD_DIM // 4
ROPE_THETA = 500000.0
NORM_EPS = 1e-6

A_HEADS = 8
A_PATTERNS = ((128, 1), (512, 4), (2048, 16))
A_W = A_HEADS * HEAD_DIM
B_HEADS = 4
B_VDIM = 2 * HEAD_DIM
B_W = B_HEADS * B_VDIM
C_HEADS = 16
C_KV_HEADS = 4
C_GROUP = C_HEADS // C_KV_HEADS
IDX_HEADS = 8
IDX_DIM = 64
TOPK_MAX = 256
D_FF = 4 * D_MODEL
C_QW = C_HEADS * HEAD_DIM
C_KVW = C_KV_HEADS * HEAD_DIM

LANES = 128
DIL_BLK = 128
NEG = -1e30
VMEM_LIMIT = 56 * 1024 * 1024

BF16 = jnp.bfloat16
F32 = jnp.float32


def _cparams(sem):
    return pltpu.CompilerParams(dimension_semantics=sem, vmem_limit_bytes=VMEM_LIMIT)


def _lo_lanes():
    return lax.broadcasted_iota(jnp.int32, (1, LANES), 1) < HEAD_DIM


def _rep(t, width):
    n = width // LANES
    return t if n == 1 else jnp.concatenate([t] * n, axis=1)


def _dot_nt(a, b):
    return lax.dot_general(a, b, (((1,), (1,)), ((), ())), preferred_element_type=F32)


def _rms(x, g):
    return x * lax.rsqrt(jnp.mean(x * x, axis=-1, keepdims=True) + NORM_EPS) * g


def _softmax_step(s, m_prev, l_prev):
    m_new = jnp.maximum(m_prev, jnp.max(s, axis=-1, keepdims=True))
    alpha = jnp.exp2(m_prev - m_new)
    p = jnp.exp2(s - _rep(m_new, s.shape[1]))
    l_new = alpha * l_prev + jnp.sum(p, axis=-1, keepdims=True)
    return p, m_new, l_new, alpha


def _rope_tables(seq_len):
    pos = jnp.arange(seq_len, dtype=F32)
    inv_freq = jnp.power(ROPE_THETA, -jnp.arange(0, ROT_DIM, 2, dtype=F32) / ROT_DIM)
    ang = pos[:, None] * inv_freq[None, :]
    cos, sin = jnp.cos(ang), jnp.sin(ang)
    half = ROT_DIM // 2
    rest = HEAD_DIM - ROT_DIM
    one = jnp.ones((seq_len, rest), F32)
    z_h = jnp.zeros((seq_len, half), F32)
    z_r = jnp.zeros((seq_len, rest), F32)
    c = jnp.concatenate([cos, cos, one], axis=1)
    sa = jnp.concatenate([z_h, sin, z_r], axis=1)
    sb = jnp.concatenate([-sin, z_h, z_r], axis=1)
    two = lambda t: jnp.concatenate([t, t], axis=1)
    return two(c), two(sa), two(sb)


def _proj_kernel(x_ref, g_ref, w_ref, c_ref, sa_ref, sb_ref, o_ref, *, chunks):
    h = _rms(x_ref[...], g_ref[...]).astype(BF16)
    c, sa, sb = c_ref[...], sa_ref[...], sb_ref[...]
    half = ROT_DIM // 2
    for start, width, rope, scale in chunks:
        acc = jnp.dot(h, w_ref[:, start:start + width], preferred_element_type=F32)
        if rope:
            acc = (acc * _rep(c, width)
                   + pltpu.roll(acc, half, 1) * _rep(sa, width)
                   + pltpu.roll(acc, width - half, 1) * _rep(sb, width))
        if scale != 1.0:
            acc = acc * scale
        o_ref[:, start:start + width] = acc.astype(o_ref.dtype)


def _norm_proj(x2, g, w, tables, chunks, seq_len, tm=256):
    m_rows, d = x2.shape
    n = w.shape[1]
    nt = seq_len // tm
    tab_spec = pl.BlockSpec((tm, LANES), lambda i: (i % nt, 0))
    return pl.pallas_call(
        functools.partial(_proj_kernel, chunks=chunks),
        out_shape=jax.ShapeDtypeStruct((m_rows, n), BF16),
        grid=(m_rows // tm,),
        in_specs=[pl.BlockSpec((tm, d), lambda i: (i, 0)),
                  pl.BlockSpec((1, d), lambda i: (0, 0)),
                  pl.BlockSpec((d, n), lambda i: (0, 0)),
                  tab_spec, tab_spec, tab_spec],
        out_specs=pl.BlockSpec((tm, n), lambda i: (i, 0)),
        compiler_params=_cparams(("parallel",)),
        name="norm_proj",
    )(x2, g.reshape(1, d), w, *tables)


def _dil_kernel(q_ref, kc_ref, kp_ref, vc_ref, vp_ref, o_ref, lse_ref):
    n = pl.program_id(2)
    blk = DIL_BLK
    row = lax.broadcasted_iota(jnp.int32, (blk, 2 * blk), 0)
    col = lax.broadcasted_iota(jnp.int32, (blk, 2 * blk), 1)
    rel = row - col + blk
    mask = (rel >= 0) & (rel <= blk) & ((col >= blk) | (n > 0))
    lo = _lo_lanes()
    for pair in range(A_W // LANES):
        sl = slice(pair * LANES, (pair + 1) * LANES)
        q = q_ref[0, :, sl]
        kcat = jnp.concatenate([kp_ref[0, :, sl], kc_ref[0, :, sl]], axis=0)
        vcat = jnp.concatenate([vp_ref[0, :, sl], vc_ref[0, :, sl]], axis=0)
        res = []
        for qm in (jnp.where(lo, q, 0), jnp.where(lo, 0, q)):
            s = jnp.where(mask, _dot_nt(qm, kcat), -jnp.inf)
            m = jnp.max(s, axis=-1, keepdims=True)
            p = jnp.exp2(s - m)
            l = jnp.sum(p, axis=-1, keepdims=True)
            pv = jnp.dot(p.astype(BF16), vcat, preferred_element_type=F32)
            res.append((pv / l, m + jnp.log2(l)))
        o_ref[0, :, sl] = jnp.where(lo, res[0][0], res[1][0])
        lse_ref[0, :, sl] = jnp.where(lo, res[0][1], res[1][1])


def _dilated_pattern(proj, dil, n_cols):
    b, s, _ = proj.shape
    ls = s // dil
    nb = ls // DIL_BLK
    cb = n_cols // A_W
    view = proj.reshape(b, ls, dil * n_cols)
    blk = (1, DIL_BLK, A_W)
    cur = lambda c: pl.BlockSpec(blk, lambda bi, r, n: (bi, n, r * cb + c))
    prev = lambda c: pl.BlockSpec(blk, lambda bi, r, n: (bi, jnp.maximum(n - 1, 0), r * cb + c))
    out_spec = pl.BlockSpec(blk, lambda bi, r, n: (bi, n, r))
    out_sds = jax.ShapeDtypeStruct((b, ls, dil * A_W), F32)
    o, lse = pl.pallas_call(
        _dil_kernel,
        out_shape=(out_sds, out_sds),
        grid=(b, dil, nb),
        in_specs=[cur(0), cur(1), prev(1), cur(2), prev(2)],
        out_specs=(out_spec, out_spec),
        compiler_params=_cparams(("parallel", "parallel", "arbitrary")),
        name=f"dilated_d{dil}",
    )(view, view, view, view, view)
    return o.reshape(b, s, A_W), lse.reshape(b, s, A_W)


def _merge_kernel(o1, o2, o3, l1, l2, l3, out_ref):
    a, b, c = l1[...], l2[...], l3[...]
    m = jnp.maximum(jnp.maximum(a, b), c)
    ea, eb, ec = jnp.exp2(a - m), jnp.exp2(b - m), jnp.exp2(c - m)
    num = ea * o1[...] + eb * o2[...] + ec * o3[...]
    out_ref[...] = (num / (ea + eb + ec)).astype(out_ref.dtype)


def _dilated_merge(outs, lses, tm=512):
    b, s, w = outs[0].shape
    m_rows = b * s
    spec = pl.BlockSpec((tm, w), lambda i: (i, 0))
    flat = [t.reshape(m_rows, w) for t in (*outs, *lses)]
    return pl.pallas_call(
        _merge_kernel,
        out_shape=jax.ShapeDtypeStruct((m_rows, w), BF16),
        grid=(m_rows // tm,),
        in_specs=[spec] * 6,
        out_specs=spec,
        compiler_params=_cparams(("parallel",)),
        name="dilated_merge",
    )(*flat)


def _diff_kernel(q_ref, k_ref, v_ref, lq1, lk1, lq2, lk2, g_ref, o_ref,
                 qs_sc, m_sc, l_sc, acc_sc, *, tq, lam_init):
    i = pl.program_id(2)
    lo = _lo_lanes()
    q = q_ref[0]
    qs_sc[...] = jnp.concatenate([jnp.where(lo, q, 0), jnp.where(lo, 0, q)], axis=0)
    m_sc[...] = jnp.full(m_sc.shape, NEG, F32)
    l_sc[...] = jnp.zeros(l_sc.shape, F32)
    acc_sc[...] = jnp.zeros(acc_sc.shape, F32)

    def step(j, masked):
        start = pl.multiple_of(j * tq, tq)
        k = k_ref[0, pl.ds(start, tq), :]
        v = v_ref[0, pl.ds(start, tq), :]
        s = _dot_nt(qs_sc[...], k)
        if masked:
            row = lax.broadcasted_iota(jnp.int32, (2 * tq, tq), 0)
            col = lax.broadcasted_iota(jnp.int32, (2 * tq, tq), 1)
            s = jnp.where(col <= jnp.where(row >= tq, row - tq, row), s, NEG)
        p, m_new, l_new, alpha = _softmax_step(s, m_sc[...], l_sc[...])
        m_sc[...] = m_new
        l_sc[...] = l_new
        acc_sc[...] = alpha * acc_sc[...] + jnp.dot(p.astype(BF16), v, preferred_element_type=F32)

    def body(j, carry):
        step(j, False)
        return carry

    lax.fori_loop(0, i, body, 0)
    step(i, True)

    lam = (jnp.exp(jnp.sum(lq1[...] * lk1[...], axis=-1, keepdims=True))
           - jnp.exp(jnp.sum(lq2[...] * lk2[...], axis=-1, keepdims=True)) + lam_init)
    o = acc_sc[...] / l_sc[...]
    o = o[:tq] - lam * o[tq:]
    o_ref[0] = (_rms(o, g_ref[...]) * (1.0 - lam_init)).astype(o_ref.dtype)


def _diff_attention(proj, lq1, lk1, lq2, lk2, subln, lam_init, tq=512):
    b, s, _ = proj.shape
    qb, kb, vb = 3 * A_W // LANES, (3 * A_W + B_W) // LANES, (3 * A_W + 2 * B_W) // LANES
    vec = pl.BlockSpec((1, HEAD_DIM), lambda bi, h, i: (0, 0))
    return pl.pallas_call(
        functools.partial(_diff_kernel, tq=tq, lam_init=lam_init),
        out_shape=jax.ShapeDtypeStruct((b, s, B_W), BF16),
        grid=(b, B_HEADS, s // tq),
        in_specs=[pl.BlockSpec((1, tq, LANES), lambda bi, h, i: (bi, i, qb + h)),
                  pl.BlockSpec((1, s, LANES), lambda bi, h, i: (bi, 0, kb + h)),
                  pl.BlockSpec((1, s, LANES), lambda bi, h, i: (bi, 0, vb + h)),
                  vec, vec, vec, vec,
                  pl.BlockSpec((1, B_VDIM), lambda bi, h, i: (0, 0))],
        out_specs=pl.BlockSpec((1, tq, LANES), lambda bi, h, i: (bi, i, h)),
        scratch_shapes=[pltpu.VMEM((2 * tq, LANES), BF16)] + [pltpu.VMEM((2 * tq, LANES), F32)] * 3,
        compiler_params=_cparams(("parallel", "parallel", "arbitrary")),
        name="diff_attention",
    )(proj, proj, proj, lq1.reshape(1, -1), lk1.reshape(1, -1), lq2.reshape(1, -1),
      lk2.reshape(1, -1), subln.reshape(1, -1))


ODD_Q = 0
ODD_K = C_QW
ODD_V = ODD_K + C_KV_HEADS * LANES
ODD_QI = ODD_V + C_KV_HEADS * LANES
ODD_KI = ODD_QI + IDX_HEADS * IDX_DIM
ODD_WI = ODD_KI + LANES
ODD_N = ODD_WI + LANES


def _stack_heads(x, lo):
    parts = []
    for pair in range(x.shape[1] // LANES):
        xp = x[:, pair * LANES:(pair + 1) * LANES]
        parts += [jnp.where(lo, xp, 0), jnp.where(lo, 0, xp)]
    return jnp.concatenate(parts, axis=0)


def _index_kernel(qi_ref, ki_ref, wi_ref, bias_ref, sc, qs_sc, cut_sc, *, tq, tk, nkc, topk, idx_bits):
    i = pl.program_id(1)
    nv = (i * tq) // tk + 1
    kf = float(topk)
    krow = lax.broadcasted_iota(jnp.int32, (tk, tq), 0)
    qpos = i * tq + lax.broadcasted_iota(jnp.int32, (tk, tq), 1)
    qs_sc[...] = _stack_heads(qi_ref[0], _lo_lanes())
    w_t = wi_ref[0].astype(F32).T

    def score_body(j, carry):
        start = pl.multiple_of(j * tk, tk)
        s = _dot_nt(ki_ref[0, pl.ds(start, tk), :], qs_sc[...])
        score = jnp.zeros((tk, tq), F32)
        for h in range(IDX_HEADS):
            score = score + jnp.maximum(s[:, h * tq:(h + 1) * tq], 0.0) * w_t[h:h + 1, :]
        sc[j] = jnp.where(start + krow <= qpos, score, -jnp.inf)
        return carry

    lax.fori_loop(0, nv, score_body, 0)

    def count(pred):
        rows = 32
        def body(j, acc):
            c = jnp.where(pred(sc[j], j * tk + krow), 1.0, 0.0)
            return acc + jnp.sum(c.reshape(tk // rows, rows, tq), axis=0)
        acc = lax.fori_loop(0, nv, body, jnp.zeros((rows, tq), F32))
        return jnp.sum(acc, axis=0, keepdims=True)

    neg = count(lambda s, _: s >= 0.0) < kf

    def bit_body(t, cur):
        cand = cur | lax.shift_left(jnp.int32(1), 30 - t)
        cf = lax.bitcast_convert_type(cand, F32)
        thr = jnp.where(neg, -cf, cf)
        cnt = count(lambda s, _: s >= thr)
        return jnp.where(jnp.logical_xor(cnt >= kf, neg), cand, cur)

    cur = lax.fori_loop(0, 31, bit_body, jnp.zeros((1, tq), jnp.int32))
    mag = lax.bitcast_convert_type(jnp.where(neg, cur + 1, cur), F32)
    thr = jnp.where(neg, -mag, mag)

    cut_sc[...] = jnp.full(cut_sc.shape, (1 << idx_bits) - 1, jnp.int32)
    n_ge = count(lambda s, _: s >= thr)

    @pl.when(jnp.max(n_ge) > kf)
    def _():
        need = kf - count(lambda s, _: s > thr)

        def idx_body(t, cur):
            cand = cur | lax.shift_left(jnp.int32(1), idx_bits - 1 - t)
            cnt = count(lambda s, kidx: (s == thr) & (kidx < cand))
            return jnp.where(cnt < need, cand, cur)

        cut = lax.fori_loop(0, idx_bits, idx_body, jnp.zeros((1, tq), jnp.int32))
        cut_sc[...] = jnp.broadcast_to(cut, cut_sc.shape)

    cut = cut_sc[0:1, :]
    for j in range(nkc):
        @pl.when(j < nv)
        def _():
            s = sc[j]
            kidx = j * tk + krow
            sel = ((s > thr) | ((s == thr) & (kidx <= cut))) & (kidx <= qpos)
            bias_ref[0, j] = jnp.where(sel, 0.0, NEG).T.astype(bias_ref.dtype)

        @pl.when(j >= nv)
        def _():
            bias_ref[0, j] = jnp.full((tq, tk), NEG, bias_ref.dtype)


def _dsa_select(proj, topk, tq=256, tk=512):
    b, s, _ = proj.shape
    nkc = s // tk
    idx_bits = max(1, (s - 1).bit_length())
    qiw = IDX_HEADS * IDX_DIM
    return pl.pallas_call(
        functools.partial(_index_kernel, tq=tq, tk=tk, nkc=nkc, topk=topk, idx_bits=idx_bits),
        out_shape=jax.ShapeDtypeStruct((b, nkc, s, tk), BF16),
        grid=(b, s // tq),
        in_specs=[pl.BlockSpec((1, tq, qiw), lambda bi, i: (bi, i, ODD_QI // qiw)),
                  pl.BlockSpec((1, s, LANES), lambda bi, i: (bi, 0, ODD_KI // LANES)),
                  pl.BlockSpec((1, tq, LANES), lambda bi, i: (bi, i, ODD_WI // LANES))],
        out_specs=pl.BlockSpec((1, nkc, tq, tk), lambda bi, i: (bi, 0, i, 0)),
        scratch_shapes=[pltpu.VMEM((nkc, tk, tq), F32),
                        pltpu.VMEM((IDX_HEADS * tq, LANES), BF16),
                        pltpu.VMEM((8, tq), jnp.int32)],
        compiler_params=_cparams(("parallel", "arbitrary")),
        name="dsa_select",
    )(proj, proj, proj)


def _dsa_kernel(q_ref, k_ref, v_ref, bias_ref, o_ref, qs_sc, m_sc, acc_sc, *, tq, tk):
    i = pl.program_id(1)
    nv = (i * tq) // tk + 1
    lo = _lo_lanes()
    lane = lax.broadcasted_iota(jnp.int32, (1, LANES), 1)
    one_col = jnp.where(lane == HEAD_DIM, 1, 0).astype(BF16)
    gw = C_GROUP * HEAD_DIM
    for g in range(C_KV_HEADS):
        qs_sc[g] = _stack_heads(q_ref[0, :, g * gw:(g + 1) * gw], lo)
    m_sc[...] = jnp.full(m_sc.shape, NEG, F32)
    acc_sc[...] = jnp.zeros(acc_sc.shape, F32)

    def body(j, carry):
        start = pl.multiple_of(j * tk, tk)
        bias = bias_ref[0, j].astype(F32)
        bias = jnp.concatenate([bias] * C_GROUP, axis=0)
        for g in range(C_KV_HEADS):
            k = k_ref[0, pl.ds(start, tk), g * LANES:(g + 1) * LANES]
            v = jnp.where(lo, v_ref[0, pl.ds(start, tk), g * LANES:(g + 1) * LANES], one_col)
            s = _dot_nt(qs_sc[g], k) + bias
            m_prev = m_sc[g]
            m_new = jnp.maximum(m_prev, jnp.max(s, axis=-1, keepdims=True))
            p = jnp.exp2(s - _rep(m_new, tk))
            m_sc[g] = m_new
            acc_sc[g] = (jnp.exp2(m_prev - m_new) * acc_sc[g]
                         + jnp.dot(p.astype(BF16), v, preferred_element_type=F32))
        return carry

    lax.fori_loop(0, nv, body, 0)
    for g in range(C_KV_HEADS):
        acc = acc_sc[g]
        o = acc / acc[:, HEAD_DIM:HEAD_DIM + 1]
        for pair in range(gw // LANES):
            r = 2 * pair * tq
            col = g * gw + pair * LANES
            odd = pltpu.roll(o[r + tq:r + 2 * tq], HEAD_DIM, 1)
            o_ref[0, :, col:col + LANES] = jnp.where(lo, o[r:r + tq], odd).astype(o_ref.dtype)


def _dsa_attention(proj, bias, tq=256, tk=512):
    b, s, _ = proj.shape
    nkc = s // tk
    kvw = C_KV_HEADS * LANES
    state = pltpu.VMEM((C_KV_HEADS, C_GROUP * tq, LANES), F32)
    return pl.pallas_call(
        functools.partial(_dsa_kernel, tq=tq, tk=tk),
        out_shape=jax.ShapeDtypeStruct((b, s, C_QW), BF16),
        grid=(b, s // tq),
        in_specs=[pl.BlockSpec((1, tq, C_QW), lambda bi, i: (bi, i, ODD_Q // C_QW)),
                  pl.BlockSpec((1, s, kvw), lambda bi, i: (bi, 0, ODD_K // kvw)),
                  pl.BlockSpec((1, s, kvw), lambda bi, i: (bi, 0, ODD_V // kvw)),
                  pl.BlockSpec((1, nkc, tq, tk), lambda bi, i: (bi, 0, i, 0))],
        out_specs=pl.BlockSpec((1, tq, C_QW), lambda bi, i: (bi, i, 0)),
        scratch_shapes=[pltpu.VMEM((C_KV_HEADS, C_GROUP * tq, LANES), BF16), state, state],
        compiler_params=_cparams(("parallel", "arbitrary")),
        name="dsa_attention",
    )(proj, proj, proj, bias)


def _out_proj_kernel(*refs):
    x_ref, o_ref = refs[0], refs[-1]
    pairs = refs[1:-1]
    acc = x_ref[...]
    for a_ref, w_ref in zip(pairs[0::2], pairs[1::2]):
        acc = acc + jnp.dot(a_ref[...], w_ref[...], preferred_element_type=F32)
    o_ref[...] = acc


def _out_proj(x2, parts, tm=512):
    m_rows, d = x2.shape
    in_specs = [pl.BlockSpec((tm, d), lambda i: (i, 0))]
    args = [x2]
    for a, w in parts:
        in_specs += [pl.BlockSpec((tm, a.shape[1]), lambda i: (i, 0)),
                     pl.BlockSpec(w.shape, lambda i: (0, 0))]
        args += [a, w]
    return pl.pallas_call(
        _out_proj_kernel,
        out_shape=jax.ShapeDtypeStruct((m_rows, d), F32),
        grid=(m_rows // tm,),
        in_specs=in_specs,
        out_specs=pl.BlockSpec((tm, d), lambda i: (i, 0)),
        compiler_params=_cparams(("parallel",)),
        name="out_proj",
    )(*args)


def _ffn_kernel(x_ref, g_ref, wu_ref, wd_ref, gf_ref, o_ref, h_sc, acc_sc, *, final_norm):
    f = pl.program_id(1)

    @pl.when(f == 0)
    def _():
        h_sc[...] = _rms(x_ref[...], g_ref[...]).astype(BF16)
        acc_sc[...] = jnp.zeros(acc_sc.shape, F32)

    u = jnp.maximum(jnp.dot(h_sc[...], wu_ref[...], preferred_element_type=F32), 0.0)
    acc_sc[...] += jnp.dot((u * u).astype(BF16), wd_ref[...], preferred_element_type=F32)

    @pl.when(f == pl.num_programs(1) - 1)
    def _():
        y = x_ref[...] + acc_sc[...]
        o_ref[...] = _rms(y, gf_ref[...]) if final_norm else y


def _ffn(x2, g, wu, wd, g_final, final_norm, tm=1024, tf=512):
    m_rows, d = x2.shape
    tm = min(tm, m_rows)
    dff = wu.shape[1]
    return pl.pallas_call(
        functools.partial(_ffn_kernel, final_norm=final_norm),
        out_shape=jax.ShapeDtypeStruct((m_rows, d), F32),
        grid=(m_rows // tm, dff // tf),
        in_specs=[pl.BlockSpec((tm, d), lambda i, f: (i, 0)),
                  pl.BlockSpec((1, d), lambda i, f: (0, 0)),
                  pl.BlockSpec((d, tf), lambda i, f: (0, f)),
                  pl.BlockSpec((tf, d), lambda i, f: (f, 0)),
                  pl.BlockSpec((1, d), lambda i, f: (0, 0))],
        out_specs=pl.BlockSpec((tm, d), lambda i, f: (i, 0)),
        scratch_shapes=[pltpu.VMEM((tm, d), BF16), pltpu.VMEM((tm, d), F32)],
        compiler_params=_cparams(("parallel", "arbitrary")),
        name="ffn",
    )(x2, g.reshape(1, d), wu, wd, g_final.reshape(1, d))


Q_SCALE = HEAD_DIM ** -0.5 * math.log2(math.e)


def _even_chunks():
    scale = Q_SCALE
    w = A_W
    return ((0, w, True, scale), (w, w, True, 1.0), (2 * w, w, False, 1.0),
            (3 * w, w, True, scale), (4 * w, w, True, 1.0), (5 * w, w, False, 1.0))


def _odd_chunks():
    w = 512
    return ((ODD_Q, w, True, Q_SCALE), (ODD_Q + w, w, True, Q_SCALE),
            (ODD_K, w, True, 1.0), (ODD_V, w, False, 1.0),
            (ODD_QI, w, True, IDX_DIM ** -0.5), (ODD_KI, LANES, True, 1.0),
            (ODD_WI, LANES, False, IDX_HEADS ** -0.5))


def _odd_weight(w):
    d = w.shape[0]
    q = w[:, :C_QW]
    k = w[:, C_QW:C_QW + C_KVW].reshape(d, C_KV_HEADS, 1, HEAD_DIM)
    v = w[:, C_QW + C_KVW:C_QW + 2 * C_KVW].reshape(d, C_KV_HEADS, 1, HEAD_DIM)
    dup = lambda t: jnp.broadcast_to(t, (d, C_KV_HEADS, 2, HEAD_DIM)).reshape(d, C_KV_HEADS * LANES)
    o = C_QW + 2 * C_KVW
    qi = w[:, o:o + IDX_HEADS * IDX_DIM]
    ki = w[:, o + IDX_HEADS * IDX_DIM:o + IDX_HEADS * IDX_DIM + IDX_DIM]
    wi = w[:, o + IDX_HEADS * IDX_DIM + IDX_DIM:]
    pad = jnp.zeros((d, LANES - IDX_HEADS), w.dtype)
    return jnp.concatenate([q, dup(k), dup(v), qi, ki, ki, wi, pad], axis=1)


def kernel(x, norm_mix, norm_ffn, w_in_even, w_out_even, lambda_q1, lambda_k1, lambda_q2,
           lambda_k2, diff_subln, w_in_odd, w_out_odd, w_ffn_up, w_ffn_down, norm_final):
    b, s, d = x.shape
    depth = norm_mix.shape[0]
    tables = _rope_tables(s)
    topk = min(TOPK_MAX, s // 4)
    x2 = x.reshape(b * s, d)
    for layer in range(depth):
        if layer % 2 == 0:
            e = layer // 2
            proj = _norm_proj(x2, norm_mix[layer], w_in_even[e].astype(BF16), tables,
                              _even_chunks(), s)
            n_cols = proj.shape[1]
            proj3 = proj.reshape(b, s, n_cols)
            stats = [_dilated_pattern(proj3, dil, n_cols) for (_, dil) in A_PATTERNS]
            out_a = _dilated_merge([st[0] for st in stats], [st[1] for st in stats])
            lam_init = 0.8 - 0.6 * math.exp(-0.3 * layer)
            out_b = _diff_attention(proj3, lambda_q1[e], lambda_k1[e], lambda_q2[e], lambda_k2[e],
                                    diff_subln[e], lam_init)
            wo = w_out_even[e].astype(BF16)
            x2 = _out_proj(x2, [(out_a, wo[:A_W]), (out_b.reshape(b * s, B_W), wo[A_W:])])
        else:
            o = layer // 2
            proj = _norm_proj(x2, norm_mix[layer], _odd_weight(w_in_odd[o]).astype(BF16), tables,
                              _odd_chunks(), s)
            proj3 = proj.reshape(b, s, proj.shape[1])
            bias = _dsa_select(proj3, topk)
            out_c = _dsa_attention(proj3, bias)
            x2 = _out_proj(x2, [(out_c.reshape(b * s, C_QW), w_out_odd[o].astype(BF16))])
        x2 = _ffn(x2, norm_ffn[layer], w_ffn_up[layer].astype(BF16), w_ffn_down[layer].astype(BF16),
                  norm_final, layer == depth - 1)
    return x2.reshape(b, s, d)
```

```python
import functools
import math

import jax
import jax.numpy as jnp
from jax import lax
from jax.experimental import pallas as pl
from jax.experimental.pallas import tpu as pltpu

D_MODEL = 1024
HEAD_DIM = 64
ROT_DIM = HEAD_DIM // 4
ROPE_THETA = 500000.0
NORM_EPS = 1e-6

A_HEADS = 8
A_PATTERNS = ((128, 1), (512, 4), (2048, 16))
A_W = A_HEADS * HEAD_DIM
B_HEADS = 4
B_VDIM = 2 * HEAD_DIM
B_W = B_HEADS * B_VDIM
C_HEADS = 16
C_KV_HEADS = 4
C_GROUP = C_HEADS // C_KV_HEADS
IDX_HEADS = 8
IDX_DIM = 64
TOPK_MAX = 256
D_FF = 4 * D_MODEL
C_QW = C_HEADS * HEAD_DIM
C_KVW = C_KV_HEADS * HEAD_DIM

LANES = 128
DIL_BLK = 128
NEG = -1e30
VMEM_LIMIT = 56 * 1024 * 1024

BF16 = jnp.bfloat16
F32 = jnp.float32


def _cparams(sem):
    return pltpu.CompilerParams(dimension_semantics=sem, vmem_limit_bytes=VMEM_LIMIT)


def _lo_lanes():
    return lax.broadcasted_iota(jnp.int32, (1, LANES), 1) < HEAD_DIM


def _rep(t, width):
    n = width // LANES
    return t if n == 1 else jnp.concatenate([t] * n, axis=1)


def _dot_nt(a, b):
    return lax.dot_general(a, b, (((1,), (1,)), ((), ())), preferred_element_type=F32)


def _rms(x, g):
    return x * lax.rsqrt(jnp.mean(x * x, axis=-1, keepdims=True) + NORM_EPS) * g


def _softmax_step(s, m_prev, l_prev):
    m_new = jnp.maximum(m_prev, jnp.max(s, axis=-1, keepdims=True))
    alpha = jnp.exp2(m_prev - m_new)
    p = jnp.exp2(s - _rep(m_new, s.shape[1]))
    l_new = alpha * l_prev + jnp.sum(p, axis=-1, keepdims=True)
    return p, m_new, l_new, alpha


def _rope_tables(seq_len):
    pos = jnp.arange(seq_len, dtype=F32)
    inv_freq = jnp.power(ROPE_THETA, -jnp.arange(0, ROT_DIM, 2, dtype=F32) / ROT_DIM)
    ang = pos[:, None] * inv_freq[None, :]
    cos, sin = jnp.cos(ang), jnp.sin(ang)
    half = ROT_DIM // 2
    rest = HEAD_DIM - ROT_DIM
    one = jnp.ones((seq_len, rest), F32)
    z_h = jnp.zeros((seq_len, half), F32)
    z_r = jnp.zeros((seq_len, rest), F32)
    c = jnp.concatenate([cos, cos, one], axis=1)
    sa = jnp.concatenate([z_h, sin, z_r], axis=1)
    sb = jnp.concatenate([-sin, z_h, z_r], axis=1)
    two = lambda t: jnp.concatenate([t, t], axis=1)
    return two(c), two(sa), two(sb)


def _proj_kernel(x_ref, g_ref, w_ref, c_ref, sa_ref, sb_ref, *o_refs, chunks):
    h = _rms(x_ref[...], g_ref[...]).astype(BF16)
    c, sa, sb = c_ref[...], sa_ref[...], sb_ref[...]
    half = ROT_DIM // 2
    for start, width, rope, scale, dest, dstart in chunks:
        acc = jnp.dot(h, w_ref[:, start:start + width], preferred_element_type=F32)
        if rope:
            acc = (acc * _rep(c, width)
                   + pltpu.roll(acc, half, 1) * _rep(sa, width)
                   + pltpu.roll(acc, width - half, 1) * _rep(sb, width))
        if scale != 1.0:
            acc = acc * scale
        o_ref = o_refs[dest]
        o_ref[:, dstart:dstart + width] = acc.astype(o_ref.dtype)


def _norm_proj(x2, g, w, tables, chunks, seq_len, outs, tm=256):
    m_rows, d = x2.shape
    n = w.shape[1]
    nt = seq_len // tm
    tab_spec = pl.BlockSpec((tm, LANES), lambda i: (i % nt, 0))
    return pl.pallas_call(
        functools.partial(_proj_kernel, chunks=chunks),
        out_shape=tuple(jax.ShapeDtypeStruct((m_rows, cols), dt) for cols, dt in outs),
        grid=(m_rows // tm,),
        in_specs=[pl.BlockSpec((tm, d), lambda i: (i, 0)),
                  pl.BlockSpec((1, d), lambda i: (0, 0)),
                  pl.BlockSpec((d, n), lambda i: (0, 0)),
                  tab_spec, tab_spec, tab_spec],
        out_specs=tuple(pl.BlockSpec((tm, cols), lambda i: (i, 0)) for cols, _ in outs),
        compiler_params=_cparams(("parallel",)),
        name="norm_proj",
    )(x2, g.reshape(1, d), w, *tables)


DIL_TILE = DIL_BLK * max(d for _, d in A_PATTERNS)
DIL_BATCH = 4


def _dil_work():
    units = []
    for _, d in A_PATTERNS:
        span = DIL_BLK * d
        for blk in range(DIL_TILE // span):
            for r in range(d):
                off = blk * span + r
                units.append((d, off, (off - span) % DIL_TILE, blk == 0))
    return [units[i:i + DIL_BATCH] for i in range(0, len(units), DIL_BATCH)]


def _dil_kernel(q_ref, kc_ref, kp_ref, vc_ref, vp_ref, o_ref, m_sc, l_sc, acc_sc):
    has_prev = pl.program_id(1) > 0
    blk = DIL_BLK
    lo = _lo_lanes()
    row = lax.broadcasted_iota(jnp.int32, (2 * blk, 2 * blk), 0)
    col = lax.broadcasted_iota(jnp.int32, (2 * blk, 2 * blk), 1)
    rel = jnp.where(row >= blk, row - blk, row) - col + blk
    band = (rel >= 0) & (rel <= blk)
    band_first = band & ((col >= blk) | has_prev)

    def rows(d, off):
        return pl.ds(off, blk, stride=d) if d > 1 else pl.ds(off, blk)

    for units in _dil_work():
        init = units[0][0] == A_PATTERNS[0][1]
        s_parts, v_parts = [], []
        for d, off, poff, prev_tile in units:
            q = q_ref[0, rows(d, off), :].astype(BF16)
            qs = jnp.concatenate([jnp.where(lo, q, 0), jnp.where(lo, 0, q)], axis=0)
            kp, vp = (kp_ref, vp_ref) if prev_tile else (kc_ref, vc_ref)
            kcat = jnp.concatenate([kp[0, rows(d, poff), :], kc_ref[0, rows(d, off), :]], axis=0)
            vcat = jnp.concatenate([vp[0, rows(d, poff), :], vc_ref[0, rows(d, off), :]], axis=0)
            s = _dot_nt(qs, kcat.astype(BF16))
            s_parts.append(jnp.where(band_first if prev_tile else band, s, NEG))
            v_parts.append(vcat.astype(BF16))
        s = jnp.concatenate(s_parts, axis=0)
        m_cur = jnp.max(s, axis=-1, keepdims=True)
        n_rows = s.shape[0]
        if init:
            m_new = jnp.broadcast_to(m_cur, (n_rows, LANES))
            p = jnp.exp2(s - m_cur)
            l_new = jnp.broadcast_to(jnp.sum(p, axis=-1, keepdims=True), (n_rows, LANES))
        else:
            state = lambda sc: jnp.concatenate(
                [sc[h, rows(d, off), :] for d, off, _, _ in units for h in range(2)], axis=0)
            m_prev = state(m_sc)
            m_new = jnp.maximum(m_prev, m_cur)
            alpha = jnp.exp2(m_prev - m_new)
            p = jnp.exp2(s - _rep(m_new, 2 * blk))
            l_new = alpha * state(l_sc) + jnp.sum(p, axis=-1, keepdims=True)
            acc_prev = state(acc_sc)
        p = p.astype(BF16)
        for u, (d, off, _, _) in enumerate(units):
            r0 = u * 2 * blk
            pv = jnp.dot(p[r0:r0 + 2 * blk], v_parts[u], preferred_element_type=F32)
            if not init:
                pv = alpha[r0:r0 + 2 * blk] * acc_prev[r0:r0 + 2 * blk] + pv
            for h in range(2):
                sl = slice(r0 + h * blk, r0 + (h + 1) * blk)
                m_sc[h, rows(d, off), :] = m_new[sl]
                l_sc[h, rows(d, off), :] = l_new[sl]
                acc_sc[h, rows(d, off), :] = pv[h * blk:(h + 1) * blk]

    o_ref[0] = jnp.where(lo, acc_sc[0] / l_sc[0], acc_sc[1] / l_sc[1]).astype(o_ref.dtype)


def _dilated_attention(qkv):
    b, s, _ = qkv.shape
    n_pairs = A_W // LANES
    blk = (1, DIL_TILE, LANES)
    cur = lambda c: pl.BlockSpec(blk, lambda bi, n, p: (bi, n, c * n_pairs + p))
    prev = lambda c: pl.BlockSpec(blk, lambda bi, n, p: (bi, jnp.maximum(n - 1, 0), c * n_pairs + p))
    state = pltpu.VMEM((2, DIL_TILE, LANES), F32)
    return pl.pallas_call(
        _dil_kernel,
        out_shape=jax.ShapeDtypeStruct((b, s, A_W), BF16),
        grid=(b, s // DIL_TILE, n_pairs),
        in_specs=[cur(0), cur(1), prev(1), cur(2), prev(2)],
        out_specs=pl.BlockSpec(blk, lambda bi, n, p: (bi, n, p)),
        scratch_shapes=[state, state, state],
        compiler_params=_cparams(("parallel", "parallel", "arbitrary")),
        name="dilated_attention",
    )(qkv, qkv, qkv, qkv, qkv)


def _diff_kernel(q_ref, k_ref, v_ref, lq1, lk1, lq2, lk2, g_ref, o_ref,
                 qs_sc, m_sc, l_sc, acc_sc, *, tq, lam_init):
    i = pl.program_id(2)
    lo = _lo_lanes()
    q = q_ref[0]
    qs_sc[...] = jnp.concatenate([jnp.where(lo, q, 0), jnp.where(lo, 0, q)], axis=0)
    m_sc[...] = jnp.full(m_sc.shape, NEG, F32)
    l_sc[...] = jnp.zeros(l_sc.shape, F32)
    acc_sc[...] = jnp.zeros(acc_sc.shape, F32)

    def step(j, masked):
        start = pl.multiple_of(j * tq, tq)
        k = k_ref[0, pl.ds(start, tq), :]
        v = v_ref[0, pl.ds(start, tq), :]
        s = _dot_nt(qs_sc[...], k)
        if masked:
            row = lax.broadcasted_iota(jnp.int32, (2 * tq, tq), 0)
            col = lax.broadcasted_iota(jnp.int32, (2 * tq, tq), 1)
            s = jnp.where(col <= jnp.where(row >= tq, row - tq, row), s, NEG)
        p, m_new, l_new, alpha = _softmax_step(s, m_sc[...], l_sc[...])
        m_sc[...] = m_new
        l_sc[...] = l_new
        acc_sc[...] = alpha * acc_sc[...] + jnp.dot(p.astype(BF16), v, preferred_element_type=F32)

    def body(j, carry):
        step(j, False)
        return carry

    lax.fori_loop(0, i, body, 0)
    step(i, True)

    lam = (jnp.exp(jnp.sum(lq1[...] * lk1[...], axis=-1, keepdims=True))
           - jnp.exp(jnp.sum(lq2[...] * lk2[...], axis=-1, keepdims=True)) + lam_init)
    o = acc_sc[...] / l_sc[...]
    o = o[:tq] - lam * o[tq:]
    o_ref[0] = (_rms(o, g_ref[...]) * (1.0 - lam_init)).astype(o_ref.dtype)


def _diff_attention(proj, lq1, lk1, lq2, lk2, subln, lam_init, tq=512):
    b, s, _ = proj.shape
    qb, kb, vb = 0, B_W // LANES, 2 * B_W // LANES
    vec = pl.BlockSpec((1, HEAD_DIM), lambda bi, h, i: (0, 0))
    return pl.pallas_call(
        functools.partial(_diff_kernel, tq=tq, lam_init=lam_init),
        out_shape=jax.ShapeDtypeStruct((b, s, B_W), BF16),
        grid=(b, B_HEADS, s // tq),
        in_specs=[pl.BlockSpec((1, tq, LANES), lambda bi, h, i: (bi, i, qb + h)),
                  pl.BlockSpec((1, s, LANES), lambda bi, h, i: (bi, 0, kb + h)),
                  pl.BlockSpec((1, s, LANES), lambda bi, h, i: (bi, 0, vb + h)),
                  vec, vec, vec, vec,
                  pl.BlockSpec((1, B_VDIM), lambda bi, h, i: (0, 0))],
        out_specs=pl.BlockSpec((1, tq, LANES), lambda bi, h, i: (bi, i, h)),
        scratch_shapes=[pltpu.VMEM((2 * tq, LANES), BF16)] + [pltpu.VMEM((2 * tq, LANES), F32)] * 3,
        compiler_params=_cparams(("parallel", "parallel", "arbitrary")),
        name="diff_attention",
    )(proj, proj, proj, lq1.reshape(1, -1), lk1.reshape(1, -1), lq2.reshape(1, -1),
      lk2.reshape(1, -1), subln.reshape(1, -1))


ODD_Q = 0
ODD_K = C_QW
ODD_V = ODD_K + C_KV_HEADS * LANES
ODD_QI = ODD_V + C_KV_HEADS * LANES
ODD_KI = ODD_QI + IDX_HEADS * IDX_DIM
ODD_WI = ODD_KI + LANES
ODD_N = ODD_WI + LANES


def _stack_heads(x, lo):
    parts = []
    for pair in range(x.shape[1] // LANES):
        xp = x[:, pair * LANES:(pair + 1) * LANES]
        parts += [jnp.where(lo, xp, 0), jnp.where(lo, 0, xp)]
    return jnp.concatenate(parts, axis=0)


def _index_kernel(qi_ref, ki_ref, wi_ref, bias_ref, sc, qs_sc, cut_sc, *, tq, tk, nkc, topk, idx_bits):
    i = pl.program_id(1)
    nv = (i * tq) // tk + 1
    kf = float(topk)
    krow = lax.broadcasted_iota(jnp.int32, (tk, tq), 0)
    qpos = i * tq + lax.broadcasted_iota(jnp.int32, (tk, tq), 1)
    qs_sc[...] = _stack_heads(qi_ref[0], _lo_lanes())
    w_t = wi_ref[0].astype(F32).T

    def score_body(j, carry):
        start = pl.multiple_of(j * tk, tk)
        s = _dot_nt(ki_ref[0, pl.ds(start, tk), :], qs_sc[...])
        score = jnp.zeros((tk, tq), F32)
        for h in range(IDX_HEADS):
            score = score + jnp.maximum(s[:, h * tq:(h + 1) * tq], 0.0) * w_t[h:h + 1, :]
        sc[j] = jnp.where(start + krow <= qpos, score, -jnp.inf)
        return carry

    lax.fori_loop(0, nv, score_body, 0)

    def count(pred):
        rows = 32
        def body(j, acc):
            c = jnp.where(pred(sc[j], j * tk + krow), 1.0, 0.0)
            return acc + jnp.sum(c.reshape(tk // rows, rows, tq), axis=0)
        acc = lax.fori_loop(0, nv, body, jnp.zeros((rows, tq), F32))
        return jnp.sum(acc, axis=0, keepdims=True)

    neg = count(lambda s, _: s >= 0.0) < kf

    def bit_body(t, cur):
        cand = cur | lax.shift_left(jnp.int32(1), 30 - t)
        cf = lax.bitcast_convert_type(cand, F32)
        thr = jnp.where(neg, -cf, cf)
        cnt = count(lambda s, _: s >= thr)
        return jnp.where(jnp.logical_xor(cnt >= kf, neg), cand, cur)

    cur = lax.fori_loop(0, 31, bit_body, jnp.zeros((1, tq), jnp.int32))
    mag = lax.bitcast_convert_type(jnp.where(neg, cur + 1, cur), F32)
    thr = jnp.where(neg, -mag, mag)

    cut_sc[...] = jnp.full(cut_sc.shape, (1 << idx_bits) - 1, jnp.int32)
    n_ge = count(lambda s, _: s >= thr)

    @pl.when(jnp.max(n_ge) > kf)
    def _():
        need = kf - count(lambda s, _: s > thr)

        def idx_body(t, cur):
            cand = cur | lax.shift_left(jnp.int32(1), idx_bits - 1 - t)
            cnt = count(lambda s, kidx: (s == thr) & (kidx < cand))
            return jnp.where(cnt < need, cand, cur)

        cut = lax.fori_loop(0, idx_bits, idx_body, jnp.zeros((1, tq), jnp.int32))
        cut_sc[...] = jnp.broadcast_to(cut, cut_sc.shape)

    cut = cut_sc[0:1, :]
    for j in range(nkc):
        @pl.when(j < nv)
        def _():
            s = sc[j]
            kidx = j * tk + krow
            sel = ((s > thr) | ((s == thr) & (kidx <= cut))) & (kidx <= qpos)
            bias_ref[0, j] = jnp.where(sel, 0.0, NEG).T.astype(bias_ref.dtype)

        @pl.when(j >= nv)
        def _():
            bias_ref[0, j] = jnp.full((tq, tk), NEG, bias_ref.dtype)


def _dsa_select(proj, topk, tq=256, tk=512):
    b, s, _ = proj.shape
    nkc = s // tk
    idx_bits = max(1, (s - 1).bit_length())
    qiw = IDX_HEADS * IDX_DIM
    return pl.pallas_call(
        functools.partial(_index_kernel, tq=tq, tk=tk, nkc=nkc, topk=topk, idx_bits=idx_bits),
        out_shape=jax.ShapeDtypeStruct((b, nkc, s, tk), BF16),
        grid=(b, s // tq),
        in_specs=[pl.BlockSpec((1, tq, qiw), lambda bi, i: (bi, i, ODD_QI // qiw)),
                  pl.BlockSpec((1, s, LANES), lambda bi, i: (bi, 0, ODD_KI // LANES)),
                  pl.BlockSpec((1, tq, LANES), lambda bi, i: (bi, i, ODD_WI // LANES))],
        out_specs=pl.BlockSpec((1, nkc, tq, tk), lambda bi, i: (bi, 0, i, 0)),
        scratch_shapes=[pltpu.VMEM((nkc, tk, tq), F32),
                        pltpu.VMEM((IDX_HEADS * tq, LANES), BF16),
                        pltpu.VMEM((8, tq), jnp.int32)],
        compiler_params=_cparams(("parallel", "arbitrary")),
        name="dsa_select",
    )(proj, proj, proj)


def _dsa_kernel(q_ref, k_ref, v_ref, bias_ref, o_ref, qs_sc, m_sc, acc_sc, *, tq, tk):
    i = pl.program_id(1)
    nv = (i * tq) // tk + 1
    lo = _lo_lanes()
    lane = lax.broadcasted_iota(jnp.int32, (1, LANES), 1)
    one_col = jnp.where(lane == HEAD_DIM, 1, 0).astype(BF16)
    gw = C_GROUP * HEAD_DIM
    for g in range(C_KV_HEADS):
        qs_sc[g] = _stack_heads(q_ref[0, :, g * gw:(g + 1) * gw], lo)
    m_sc[...] = jnp.full(m_sc.shape, NEG, F32)
    acc_sc[...] = jnp.zeros(acc_sc.shape, F32)

    def body(j, carry):
        start = pl.multiple_of(j * tk, tk)
        bias = bias_ref[0, j].astype(F32)
        bias = jnp.concatenate([bias] * C_GROUP, axis=0)
        for g in range(C_KV_HEADS):
            k = k_ref[0, pl.ds(start, tk), g * LANES:(g + 1) * LANES]
            v = jnp.where(lo, v_ref[0, pl.ds(start, tk), g * LANES:(g + 1) * LANES], one_col)
            s = _dot_nt(qs_sc[g], k) + bias
            m_prev = m_sc[g]
            m_new = jnp.maximum(m_prev, jnp.max(s, axis=-1, keepdims=True))
            p = jnp.exp2(s - _rep(m_new, tk))
            m_sc[g] = m_new
            acc_sc[g] = (jnp.exp2(m_prev - m_new) * acc_sc[g]
                         + jnp.dot(p.astype(BF16), v, preferred_element_type=F32))
        return carry

    lax.fori_loop(0, nv, body, 0)
    for g in range(C_KV_HEADS):
        acc = acc_sc[g]
        o = acc / acc[:, HEAD_DIM:HEAD_DIM + 1]
        for pair in range(gw // LANES):
            r = 2 * pair * tq
            col = g * gw + pair * LANES
            odd = pltpu.roll(o[r + tq:r + 2 * tq], HEAD_DIM, 1)
            o_ref[0, :, col:col + LANES] = jnp.where(lo, o[r:r + tq], odd).astype(o_ref.dtype)


def _dsa_attention(proj, bias, tq=256, tk=512):
    b, s, _ = proj.shape
    nkc = s // tk
    kvw = C_KV_HEADS * LANES
    state = pltpu.VMEM((C_KV_HEADS, C_GROUP * tq, LANES), F32)
    return pl.pallas_call(
        functools.partial(_dsa_kernel, tq=tq, tk=tk),
        out_shape=jax.ShapeDtypeStruct((b, s, C_QW), BF16),
        grid=(b, s // tq),
        in_specs=[pl.BlockSpec((1, tq, C_QW), lambda bi, i: (bi, i, ODD_Q // C_QW)),
                  pl.BlockSpec((1, s, kvw), lambda bi, i: (bi, 0, ODD_K // kvw)),
                  pl.BlockSpec((1, s, kvw), lambda bi, i: (bi, 0, ODD_V // kvw)),
                  pl.BlockSpec((1, nkc, tq, tk), lambda bi, i: (bi, 0, i, 0))],
        out_specs=pl.BlockSpec((1, tq, C_QW), lambda bi, i: (bi, i, 0)),
        scratch_shapes=[pltpu.VMEM((C_KV_HEADS, C_GROUP * tq, LANES), BF16), state, state],
        compiler_params=_cparams(("parallel", "arbitrary")),
        name="dsa_attention",
    )(proj, proj, proj, bias)


def _mix_ffn_kernel(*refs, n_parts, final_norm):
    x_ref = refs[0]
    parts = refs[1:1 + 2 * n_parts]
    g_ref, wu_ref, wd_ref, gf_ref, o_ref, x1_sc, h_sc, acc_sc = refs[1 + 2 * n_parts:]
    f = pl.program_id(1)

    @pl.when(f == 0)
    def _():
        x1 = x_ref[...]
        for a_ref, w_ref in zip(parts[0::2], parts[1::2]):
            x1 = x1 + jnp.dot(a_ref[...], w_ref[...], preferred_element_type=F32)
        x1_sc[...] = x1
        h_sc[...] = _rms(x1, g_ref[...]).astype(BF16)
        acc_sc[...] = jnp.zeros(acc_sc.shape, F32)

    u = jnp.maximum(jnp.dot(h_sc[...], wu_ref[...], preferred_element_type=F32), 0.0)
    acc_sc[...] += jnp.dot((u * u).astype(BF16), wd_ref[...], preferred_element_type=F32)

    @pl.when(f == pl.num_programs(1) - 1)
    def _():
        y = x1_sc[...] + acc_sc[...]
        o_ref[...] = _rms(y, gf_ref[...]) if final_norm else y


def _mix_ffn(x2, parts, g, wu, wd, g_final, final_norm, tm=1024, tf=512):
    m_rows, d = x2.shape
    tm = min(tm, m_rows)
    dff = wu.shape[1]
    row = lambda cols: pl.BlockSpec((tm, cols), lambda i, f: (i, 0))
    const = lambda shape: pl.BlockSpec(shape, lambda i, f: (0, 0))
    in_specs = [row(d)]
    args = [x2]
    for a, w in parts:
        in_specs += [row(a.shape[1]), const(w.shape)]
        args += [a, w]
    in_specs += [const((1, d)),
                 pl.BlockSpec((d, tf), lambda i, f: (0, f)),
                 pl.BlockSpec((tf, d), lambda i, f: (f, 0)),
                 const((1, d))]
    args += [g.reshape(1, d), wu, wd, g_final.reshape(1, d)]
    return pl.pallas_call(
        functools.partial(_mix_ffn_kernel, n_parts=len(parts), final_norm=final_norm),
        out_shape=jax.ShapeDtypeStruct((m_rows, d), F32),
        grid=(m_rows // tm, dff // tf),
        in_specs=in_specs,
        out_specs=row(d),
        scratch_shapes=[pltpu.VMEM((tm, d), F32), pltpu.VMEM((tm, d), BF16), pltpu.VMEM((tm, d), F32)],
        compiler_params=_cparams(("parallel", "arbitrary")),
        name="mix_ffn",
    )(*args)


Q_SCALE = HEAD_DIM ** -0.5 * math.log2(math.e)


def _even_chunks():
    scale = Q_SCALE
    w = A_W
    return ((0, w, True, scale, 1, 0), (w, w, True, 1.0, 1, w), (2 * w, w, False, 1.0, 1, 2 * w),
            (3 * w, w, True, scale, 0, 0), (4 * w, w, True, 1.0, 0, w), (5 * w, w, False, 1.0, 0, 2 * w))


def _odd_chunks():
    w = 512
    chunks = ((ODD_Q, w, True, Q_SCALE), (ODD_Q + w, w, True, Q_SCALE),
              (ODD_K, w, True, 1.0), (ODD_V, w, False, 1.0),
              (ODD_QI, w, True, IDX_DIM ** -0.5), (ODD_KI, LANES, True, 1.0),
              (ODD_WI, LANES, False, IDX_HEADS ** -0.5))
    return tuple(c + (0, c[0]) for c in chunks)


def _odd_weight(w):
    d = w.shape[0]
    q = w[:, :C_QW]
    k = w[:, C_QW:C_QW + C_KVW].reshape(d, C_KV_HEADS, 1, HEAD_DIM)
    v = w[:, C_QW + C_KVW:C_QW + 2 * C_KVW].reshape(d, C_KV_HEADS, 1, HEAD_DIM)
    dup = lambda t: jnp.broadcast_to(t, (d, C_KV_HEADS, 2, HEAD_DIM)).reshape(d, C_KV_HEADS * LANES)
    o = C_QW + 2 * C_KVW
    qi = w[:, o:o + IDX_HEADS * IDX_DIM]
    ki = w[:, o + IDX_HEADS * IDX_DIM:o + IDX_HEADS * IDX_DIM + IDX_DIM]
    wi = w[:, o + IDX_HEADS * IDX_DIM + IDX_DIM:]
    pad = jnp.zeros((d, LANES - IDX_HEADS), w.dtype)
    return jnp.concatenate([q, dup(k), dup(v), qi, ki, ki, wi, pad], axis=1)


def kernel(x, norm_mix, norm_ffn, w_in_even, w_out_even, lambda_q1, lambda_k1, lambda_q2,
           lambda_k2, diff_subln, w_in_odd, w_out_odd, w_ffn_up, w_ffn_down, norm_final):
    b, s, d = x.shape
    depth = norm_mix.shape[0]
    tables = _rope_tables(s)
    topk = min(TOPK_MAX, s // 4)
    x2 = x.reshape(b * s, d)
    for layer in range(depth):
        if layer % 2 == 0:
            e = layer // 2
            proj_b, proj_a = _norm_proj(x2, norm_mix[layer], w_in_even[e].astype(BF16), tables,
                                        _even_chunks(), s, ((3 * B_W, BF16), (3 * A_W, F32)))
            out_a = _dilated_attention(proj_a.reshape(b, s, 3 * A_W))
            lam_init = 0.8 - 0.6 * math.exp(-0.3 * layer)
            out_b = _diff_attention(proj_b.reshape(b, s, 3 * B_W), lambda_q1[e], lambda_k1[e],
                                    lambda_q2[e], lambda_k2[e], diff_subln[e], lam_init)
            wo = w_out_even[e].astype(BF16)
            parts = [(out_a.reshape(b * s, A_W), wo[:A_W]), (out_b.reshape(b * s, B_W), wo[A_W:])]
        else:
            o = layer // 2
            (proj,) = _norm_proj(x2, norm_mix[layer], _odd_weight(w_in_odd[o]).astype(BF16), tables,
                                 _odd_chunks(), s, ((ODD_N, BF16),))
            proj3 = proj.reshape(b, s, proj.shape[1])
            bias = _dsa_select(proj3, topk)
            out_c = _dsa_attention(proj3, bias)
            parts = [(out_c.reshape(b * s, C_QW), w_out_odd[o].astype(BF16))]
        x2 = _mix_ffn(x2, parts, norm_ffn[layer], w_ffn_up[layer].astype(BF16),
                      w_ffn_down[layer].astype(BF16), norm_final, layer == depth - 1)
    return x2.reshape(b, s, d)
```

```python
import functools
import math

import jax
import jax.numpy as jnp
from jax import lax
from jax.experimental import pallas as pl
from jax.experimental.pallas import tpu as pltpu

D_MODEL = 1024
HEAD_DIM = 64
ROT_DIM = HEAD_DIM // 4
ROPE_THETA = 500000.0
NORM_EPS = 1e-6

A_HEADS = 8
A_PATTERNS = ((128, 1), (512, 4), (2048, 16))
A_W = A_HEADS * HEAD_DIM
B_HEADS = 4
B_VDIM = 2 * HEAD_DIM
B_W = B_HEADS * B_VDIM
C_HEADS = 16
C_KV_HEADS = 4
C_GROUP = C_HEADS // C_KV_HEADS
IDX_HEADS = 8
IDX_DIM = 64
TOPK_MAX = 256
D_FF = 4 * D_MODEL
C_QW = C_HEADS * HEAD_DIM
C_KVW = C_KV_HEADS * HEAD_DIM

LANES = 128
DIL_BLK = 128
NEG = -1e30
VMEM_LIMIT = 56 * 1024 * 1024

BF16 = jnp.bfloat16
F32 = jnp.float32


def _cparams(sem):
    return pltpu.CompilerParams(dimension_semantics=sem, vmem_limit_bytes=VMEM_LIMIT)


def _lo_lanes():
    return lax.broadcasted_iota(jnp.int32, (1, LANES), 1) < HEAD_DIM


def _rep(t, width):
    n = width // LANES
    return t if n == 1 else jnp.concatenate([t] * n, axis=1)


def _dot_nt(a, b):
    return lax.dot_general(a, b, (((1,), (1,)), ((), ())), preferred_element_type=F32)


def _rms(x, g):
    return x * lax.rsqrt(jnp.mean(x * x, axis=-1, keepdims=True) + NORM_EPS) * g


def _softmax_step(s, m_prev, l_prev):
    m_new = jnp.maximum(m_prev, jnp.max(s, axis=-1, keepdims=True))
    alpha = jnp.exp2(m_prev - m_new)
    p = jnp.exp2(s - _rep(m_new, s.shape[1]))
    l_new = alpha * l_prev + jnp.sum(p, axis=-1, keepdims=True)
    return p, m_new, l_new, alpha


def _rope_tables(seq_len):
    pos = jnp.arange(seq_len, dtype=F32)
    inv_freq = jnp.power(ROPE_THETA, -jnp.arange(0, ROT_DIM, 2, dtype=F32) / ROT_DIM)
    ang = pos[:, None] * inv_freq[None, :]
    cos, sin = jnp.cos(ang), jnp.sin(ang)
    half = ROT_DIM // 2
    rest = HEAD_DIM - ROT_DIM
    one = jnp.ones((seq_len, rest), F32)
    z_h = jnp.zeros((seq_len, half), F32)
    z_r = jnp.zeros((seq_len, rest), F32)
    c = jnp.concatenate([cos, cos, one], axis=1)
    sa = jnp.concatenate([z_h, sin, z_r], axis=1)
    sb = jnp.concatenate([-sin, z_h, z_r], axis=1)
    two = lambda t: jnp.concatenate([t, t], axis=1)
    return two(c), two(sa), two(sb)


def _proj_kernel(x_ref, g_ref, w_ref, c_ref, sa_ref, sb_ref, *o_refs, chunks):
    h = _rms(x_ref[...], g_ref[...]).astype(BF16)
    c, sa, sb = c_ref[...], sa_ref[...], sb_ref[...]
    half = ROT_DIM // 2
    for start, width, rope, scale, dest, dstart in chunks:
        acc = jnp.dot(h, w_ref[:, start:start + width], preferred_element_type=F32)
        if rope:
            acc = (acc * _rep(c, width)
                   + pltpu.roll(acc, half, 1) * _rep(sa, width)
                   + pltpu.roll(acc, width - half, 1) * _rep(sb, width))
        if scale != 1.0:
            acc = acc * scale
        o_ref = o_refs[dest]
        o_ref[:, dstart:dstart + width] = acc.astype(o_ref.dtype)


def _norm_proj(x2, g, w, tables, chunks, seq_len, outs, tm=256):
    m_rows, d = x2.shape
    n = w.shape[1]
    nt = seq_len // tm
    tab_spec = pl.BlockSpec((tm, LANES), lambda i: (i % nt, 0))
    return pl.pallas_call(
        functools.partial(_proj_kernel, chunks=chunks),
        out_shape=tuple(jax.ShapeDtypeStruct((m_rows, cols), dt) for cols, dt in outs),
        grid=(m_rows // tm,),
        in_specs=[pl.BlockSpec((tm, d), lambda i: (i, 0)),
                  pl.BlockSpec((1, d), lambda i: (0, 0)),
                  pl.BlockSpec((d, n), lambda i: (0, 0)),
                  tab_spec, tab_spec, tab_spec],
        out_specs=tuple(pl.BlockSpec((tm, cols), lambda i: (i, 0)) for cols, _ in outs),
        compiler_params=_cparams(("parallel",)),
        name="norm_proj",
    )(x2, g.reshape(1, d), w, *tables)


DIL_TILE = DIL_BLK * max(d for _, d in A_PATTERNS)
DIL_BATCH = 4


def _dil_work():
    units = []
    for _, d in A_PATTERNS:
        span = DIL_BLK * d
        for blk in range(DIL_TILE // span):
            for r in range(d):
                off = blk * span + r
                units.append((d, off, (off - span) % DIL_TILE, blk == 0))
    return [units[i:i + DIL_BATCH] for i in range(0, len(units), DIL_BATCH)]


def _dil_kernel(q_ref, kc_ref, kp_ref, vc_ref, vp_ref, o_ref, m_sc, l_sc, acc_sc):
    has_prev = pl.program_id(1) > 0
    blk = DIL_BLK
    lo = _lo_lanes()
    row = lax.broadcasted_iota(jnp.int32, (2 * blk, 2 * blk), 0)
    col = lax.broadcasted_iota(jnp.int32, (2 * blk, 2 * blk), 1)
    rel = jnp.where(row >= blk, row - blk, row) - col + blk
    band = (rel >= 0) & (rel <= blk)
    band_first = band & ((col >= blk) | has_prev)

    def rows(d, off):
        return pl.ds(off, blk, stride=d) if d > 1 else pl.ds(off, blk)

    for units in _dil_work():
        init = units[0][0] == A_PATTERNS[0][1]
        s_parts, v_parts = [], []
        for d, off, poff, prev_tile in units:
            q = q_ref[0, rows(d, off), :].astype(BF16)
            qs = jnp.concatenate([jnp.where(lo, q, 0), jnp.where(lo, 0, q)], axis=0)
            kp, vp = (kp_ref, vp_ref) if prev_tile else (kc_ref, vc_ref)
            kcat = jnp.concatenate([kp[0, rows(d, poff), :], kc_ref[0, rows(d, off), :]], axis=0)
            vcat = jnp.concatenate([vp[0, rows(d, poff), :], vc_ref[0, rows(d, off), :]], axis=0)
            s = _dot_nt(qs, kcat.astype(BF16))
            s_parts.append(jnp.where(band_first if prev_tile else band, s, NEG))
            v_parts.append(vcat.astype(BF16))
        s = jnp.concatenate(s_parts, axis=0)
        m_cur = jnp.max(s, axis=-1, keepdims=True)
        n_rows = s.shape[0]
        if init:
            m_new = jnp.broadcast_to(m_cur, (n_rows, LANES))
            p = jnp.exp2(s - m_cur)
            l_new = jnp.broadcast_to(jnp.sum(p, axis=-1, keepdims=True), (n_rows, LANES))
        else:
            state = lambda sc: jnp.concatenate(
                [sc[h, rows(d, off), :] for d, off, _, _ in units for h in range(2)], axis=0)
            m_prev = state(m_sc)
            m_new = jnp.maximum(m_prev, m_cur)
            alpha = jnp.exp2(m_prev - m_new)
            p = jnp.exp2(s - _rep(m_new, 2 * blk))
            l_new = alpha * state(l_sc) + jnp.sum(p, axis=-1, keepdims=True)
            acc_prev = state(acc_sc)
        p = p.astype(BF16)
        for u, (d, off, _, _) in enumerate(units):
            r0 = u * 2 * blk
            pv = jnp.dot(p[r0:r0 + 2 * blk], v_parts[u], preferred_element_type=F32)
            if not init:
                pv = alpha[r0:r0 + 2 * blk] * acc_prev[r0:r0 + 2 * blk] + pv
            for h in range(2):
                sl = slice(r0 + h * blk, r0 + (h + 1) * blk)
                m_sc[h, rows(d, off), :] = m_new[sl]
                l_sc[h, rows(d, off), :] = l_new[sl]
                acc_sc[h, rows(d, off), :] = pv[h * blk:(h + 1) * blk]

    o_ref[0] = jnp.where(lo, acc_sc[0] / l_sc[0], acc_sc[1] / l_sc[1]).astype(o_ref.dtype)


def _dilated_attention(qkv):
    b, s, _ = qkv.shape
    n_pairs = A_W // LANES
    blk = (1, DIL_TILE, LANES)
    cur = lambda c: pl.BlockSpec(blk, lambda bi, n, p: (bi, n, c * n_pairs + p))
    prev = lambda c: pl.BlockSpec(blk, lambda bi, n, p: (bi, jnp.maximum(n - 1, 0), c * n_pairs + p))
    state = pltpu.VMEM((2, DIL_TILE, LANES), F32)
    return pl.pallas_call(
        _dil_kernel,
        out_shape=jax.ShapeDtypeStruct((b, s, A_W), BF16),
        grid=(b, s // DIL_TILE, n_pairs),
        in_specs=[cur(0), cur(1), prev(1), cur(2), prev(2)],
        out_specs=pl.BlockSpec(blk, lambda bi, n, p: (bi, n, p)),
        scratch_shapes=[state, state, state],
        compiler_params=_cparams(("parallel", "parallel", "arbitrary")),
        name="dilated_attention",
    )(qkv, qkv, qkv, qkv, qkv)


def _diff_kernel(q_ref, k_ref, v_ref, lq1, lk1, lq2, lk2, g_ref, o_ref,
                 qs_sc, m_sc, l_sc, acc_sc, *, tq, lam_init):
    i = pl.program_id(2)
    lo = _lo_lanes()
    q = q_ref[0]
    qs_sc[...] = jnp.concatenate([jnp.where(lo, q, 0), jnp.where(lo, 0, q)], axis=0)
    m_sc[...] = jnp.full(m_sc.shape, NEG, F32)
    l_sc[...] = jnp.zeros(l_sc.shape, F32)
    acc_sc[...] = jnp.zeros(acc_sc.shape, F32)

    def step(j, masked):
        start = pl.multiple_of(j * tq, tq)
        k = k_ref[0, pl.ds(start, tq), :]
        v = v_ref[0, pl.ds(start, tq), :]
        s = _dot_nt(qs_sc[...], k)
        if masked:
            row = lax.broadcasted_iota(jnp.int32, (2 * tq, tq), 0)
            col = lax.broadcasted_iota(jnp.int32, (2 * tq, tq), 1)
            s = jnp.where(col <= jnp.where(row >= tq, row - tq, row), s, NEG)
        p, m_new, l_new, alpha = _softmax_step(s, m_sc[...], l_sc[...])
        m_sc[...] = m_new
        l_sc[...] = l_new
        acc_sc[...] = alpha * acc_sc[...] + jnp.dot(p.astype(BF16), v, preferred_element_type=F32)

    def body(j, carry):
        step(j, False)
        return carry

    lax.fori_loop(0, i, body, 0)
    step(i, True)

    lam = (jnp.exp(jnp.sum(lq1[...] * lk1[...], axis=-1, keepdims=True))
           - jnp.exp(jnp.sum(lq2[...] * lk2[...], axis=-1, keepdims=True)) + lam_init)
    o = acc_sc[...] / l_sc[...]
    o = o[:tq] - lam * o[tq:]
    o_ref[0] = (_rms(o, g_ref[...]) * (1.0 - lam_init)).astype(o_ref.dtype)


def _diff_attention(proj, lq1, lk1, lq2, lk2, subln, lam_init, tq=512):
    b, s, _ = proj.shape
    qb, kb, vb = 0, B_W // LANES, 2 * B_W // LANES
    vec = pl.BlockSpec((1, HEAD_DIM), lambda bi, h, i: (0, 0))
    return pl.pallas_call(
        functools.partial(_diff_kernel, tq=tq, lam_init=lam_init),
        out_shape=jax.ShapeDtypeStruct((b, s, B_W), BF16),
        grid=(b, B_HEADS, s // tq),
        in_specs=[pl.BlockSpec((1, tq, LANES), lambda bi, h, i: (bi, i, qb + h)),
                  pl.BlockSpec((1, s, LANES), lambda bi, h, i: (bi, 0, kb + h)),
                  pl.BlockSpec((1, s, LANES), lambda bi, h, i: (bi, 0, vb + h)),
                  vec, vec, vec, vec,
                  pl.BlockSpec((1, B_VDIM), lambda bi, h, i: (0, 0))],
        out_specs=pl.BlockSpec((1, tq, LANES), lambda bi, h, i: (bi, i, h)),
        scratch_shapes=[pltpu.VMEM((2 * tq, LANES), BF16)] + [pltpu.VMEM((2 * tq, LANES), F32)] * 3,
        compiler_params=_cparams(("parallel", "parallel", "arbitrary")),
        name="diff_attention",
    )(proj, proj, proj, lq1.reshape(1, -1), lk1.reshape(1, -1), lq2.reshape(1, -1),
      lk2.reshape(1, -1), subln.reshape(1, -1))


ODD_Q = 0
ODD_K = C_QW
ODD_V = ODD_K + C_KV_HEADS * LANES
ODD_QI = ODD_V + C_KV_HEADS * LANES
ODD_KI = ODD_QI + IDX_HEADS * IDX_DIM
ODD_WI = ODD_KI + LANES
ODD_N = ODD_WI + LANES


def _stack_heads(x, lo):
    parts = []
    for pair in range(x.shape[1] // LANES):
        xp = x[:, pair * LANES:(pair + 1) * LANES]
        parts += [jnp.where(lo, xp, 0), jnp.where(lo, 0, xp)]
    return jnp.concatenate(parts, axis=0)


WORD_BITS = 32
PLANE_ROWS = 8 * WORD_BITS
_SWAPS = ((16, 0x0000FFFF), (8, 0x00FF00FF), (4, 0x0F0F0F0F), (2, 0x33333333), (1, 0x55555555))


def _transpose32(words):
    a = list(words)
    for j, m in _SWAPS:
        for k in range(WORD_BITS):
            if k & j == 0:
                t = (lax.shift_right_logical(a[k], jnp.int32(j)) ^ a[k + j]) & jnp.int32(m)
                a[k + j] = a[k + j] ^ t
                a[k] = a[k] ^ lax.shift_left(t, jnp.int32(j))
    return a


def _index_kernel(qi_ref, ki_ref, wi_ref, bias_ref, qs_sc, p_sc, c_sc, g_sc, t_sc,
                  *, tq, tk, nkc, topk, idx_bits):
    i = pl.program_id(1)
    nv = (i * tq) // tk + 1
    ppc = tk // PLANE_ROWS
    npv = nv * ppc
    kf = float(topk)
    krow = lax.broadcasted_iota(jnp.int32, (tk, tq), 0)
    qpos = i * tq + lax.broadcasted_iota(jnp.int32, (tk, tq), 1)
    srow = lax.broadcasted_iota(jnp.int32, (8, tq), 0)
    qs_sc[...] = _stack_heads(qi_ref[0], _lo_lanes())
    w_t = wi_ref[0].astype(F32).T

    def score_body(j, carry):
        start = pl.multiple_of(j * tk, tk)
        s = _dot_nt(ki_ref[0, pl.ds(start, tk), :], qs_sc[...])
        score = jnp.zeros((tk, tq), F32)
        for h in range(IDX_HEADS):
            score = score + jnp.maximum(s[:, h * tq:(h + 1) * tq], 0.0) * w_t[h:h + 1, :]
        score = jnp.where(start + krow <= qpos, score, -jnp.inf)
        bits = lax.bitcast_convert_type(score, jnp.int32)
        key = bits ^ (lax.shift_right_arithmetic(bits, jnp.int32(31)) | jnp.int32(-2 ** 31))
        for t in range(ppc):
            r0 = t * PLANE_ROWS
            planes = _transpose32([key[r0 + v * 8:r0 + v * 8 + 8, :] for v in range(WORD_BITS)])
            for b in range(WORD_BITS):
                p_sc[b, j * ppc + t] = planes[b]
        return carry

    lax.fori_loop(0, nv, score_body, 0)

    def lane_count(word_fn):
        acc = lax.fori_loop(0, npv, lambda pv, a: a + lax.population_count(word_fn(pv)),
                            jnp.zeros((8, tq), jnp.int32))
        return jnp.sum(acc.astype(F32), axis=0, keepdims=True)

    c_sc[...] = jnp.full(c_sc.shape, -1, jnp.int32)
    g_sc[...] = jnp.zeros(g_sc.shape, jnp.int32)

    def bit_body(t, above):
        b = WORD_BITS - 1 - t
        ones = lane_count(lambda pv: c_sc[pv] & p_sc[b, pv])
        take = above + ones >= kf
        takem = jnp.where(take, -1, 0).astype(jnp.int32)

        def update(pv, carry):
            c = c_sc[pv]
            hi = c & p_sc[b, pv]
            c_sc[pv] = (hi & takem) | ((c ^ hi) & ~takem)
            g_sc[pv] = g_sc[pv] | (hi & ~takem)
            return carry

        lax.fori_loop(0, npv, update, 0)
        return jnp.where(take, above, above + ones)

    above = lax.fori_loop(0, WORD_BITS, bit_body, jnp.zeros((1, tq), F32))

    def below(pv, c):
        nb = jnp.clip(lax.shift_right_arithmetic(c - pv * PLANE_ROWS - srow + 7, jnp.int32(3)), 0, WORD_BITS)
        return jnp.where(nb >= WORD_BITS, -1, lax.shift_left(jnp.int32(1), nb) - 1)

    t_sc[...] = jnp.full(t_sc.shape, -1, jnp.int32)
    need = kf - above
    n_tie = lane_count(lambda pv: c_sc[pv])

    @pl.when(jnp.max(n_tie - need) > 0.0)
    def _():
        def idx_body(t, cur):
            cand = cur | lax.shift_left(jnp.int32(1), idx_bits - 1 - t)
            cnt = lane_count(lambda pv: c_sc[pv] & below(pv, cand))
            return jnp.where(cnt < need, cand, cur)

        cut = lax.fori_loop(0, idx_bits, idx_body, jnp.zeros((1, tq), jnp.int32))

        def keep(pv, carry):
            t_sc[pv] = below(pv, cut + 1)
            return carry

        lax.fori_loop(0, npv, keep, 0)

    for j in range(nkc):
        @pl.when(j < nv)
        def _():
            pieces = []
            for t in range(ppc):
                pv = j * ppc + t
                sel = g_sc[pv] | (c_sc[pv] & t_sc[pv])
                pieces += [lax.shift_right_logical(sel, jnp.int32(v)) & 1 for v in range(WORD_BITS)]
            sel = jnp.concatenate(pieces, axis=0)
            ok = (sel != 0) & (j * tk + krow <= qpos)
            bias_ref[0, j] = jnp.where(ok, 0.0, NEG).T.astype(bias_ref.dtype)

        @pl.when(j >= nv)
        def _():
            bias_ref[0, j] = jnp.full((tq, tk), NEG, bias_ref.dtype)


def _dsa_select(proj, topk, tq=256, tk=512):
    b, s, _ = proj.shape
    nkc = s // tk
    idx_bits = max(1, (s - 1).bit_length())
    qiw = IDX_HEADS * IDX_DIM
    words = pltpu.VMEM((s // PLANE_ROWS, 8, tq), jnp.int32)
    return pl.pallas_call(
        functools.partial(_index_kernel, tq=tq, tk=tk, nkc=nkc, topk=topk, idx_bits=idx_bits),
        out_shape=jax.ShapeDtypeStruct((b, nkc, s, tk), BF16),
        grid=(b, s // tq),
        in_specs=[pl.BlockSpec((1, tq, qiw), lambda bi, i: (bi, i, ODD_QI // qiw)),
                  pl.BlockSpec((1, s, LANES), lambda bi, i: (bi, 0, ODD_KI // LANES)),
                  pl.BlockSpec((1, tq, LANES), lambda bi, i: (bi, i, ODD_WI // LANES))],
        out_specs=pl.BlockSpec((1, nkc, tq, tk), lambda bi, i: (bi, 0, i, 0)),
        scratch_shapes=[pltpu.VMEM((IDX_HEADS * tq, LANES), BF16),
                        pltpu.VMEM((WORD_BITS, s // PLANE_ROWS, 8, tq), jnp.int32),
                        words, words, words],
        compiler_params=_cparams(("parallel", "arbitrary")),
        name="dsa_select",
    )(proj, proj, proj)


def _dsa_kernel(q_ref, k_ref, v_ref, bias_ref, o_ref, qs_sc, m_sc, acc_sc, *, tq, tk):
    i = pl.program_id(1)
    nv = (i * tq) // tk + 1
    lo = _lo_lanes()
    lane = lax.broadcasted_iota(jnp.int32, (1, LANES), 1)
    one_col = jnp.where(lane == HEAD_DIM, 1, 0).astype(BF16)
    gw = C_GROUP * HEAD_DIM
    for g in range(C_KV_HEADS):
        qs_sc[g] = _stack_heads(q_ref[0, :, g * gw:(g + 1) * gw], lo)
    m_sc[...] = jnp.full(m_sc.shape, NEG, F32)
    acc_sc[...] = jnp.zeros(acc_sc.shape, F32)

    def body(j, carry):
        start = pl.multiple_of(j * tk, tk)
        bias = bias_ref[0, j].astype(F32)
        bias = jnp.concatenate([bias] * C_GROUP, axis=0)
        for g in range(C_KV_HEADS):
            k = k_ref[0, pl.ds(start, tk), g * LANES:(g + 1) * LANES]
            v = jnp.where(lo, v_ref[0, pl.ds(start, tk), g * LANES:(g + 1) * LANES], one_col)
            s = _dot_nt(qs_sc[g], k) + bias
            m_prev = m_sc[g]
            m_new = jnp.maximum(m_prev, jnp.max(s, axis=-1, keepdims=True))
            p = jnp.exp2(s - _rep(m_new, tk))
            m_sc[g] = m_new
            acc_sc[g] = (jnp.exp2(m_prev - m_new) * acc_sc[g]
                         + jnp.dot(p.astype(BF16), v, preferred_element_type=F32))
        return carry

    lax.fori_loop(0, nv, body, 0)
    for g in range(C_KV_HEADS):
        acc = acc_sc[g]
        o = acc / acc[:, HEAD_DIM:HEAD_DIM + 1]
        for pair in range(gw // LANES):
            r = 2 * pair * tq
            col = g * gw + pair * LANES
            odd = pltpu.roll(o[r + tq:r + 2 * tq], HEAD_DIM, 1)
            o_ref[0, :, col:col + LANES] = jnp.where(lo, o[r:r + tq], odd).astype(o_ref.dtype)


def _dsa_attention(proj, bias, tq=256, tk=512):
    b, s, _ = proj.shape
    nkc = s // tk
    kvw = C_KV_HEADS * LANES
    state = pltpu.VMEM((C_KV_HEADS, C_GROUP * tq, LANES), F32)
    return pl.pallas_call(
        functools.partial(_dsa_kernel, tq=tq, tk=tk),
        out_shape=jax.ShapeDtypeStruct((b, s, C_QW), BF16),
        grid=(b, s // tq),
        in_specs=[pl.BlockSpec((1, tq, C_QW), lambda bi, i: (bi, i, ODD_Q // C_QW)),
                  pl.BlockSpec((1, s, kvw), lambda bi, i: (bi, 0, ODD_K // kvw)),
                  pl.BlockSpec((1, s, kvw), lambda bi, i: (bi, 0, ODD_V // kvw)),
                  pl.BlockSpec((1, nkc, tq, tk), lambda bi, i: (bi, 0, i, 0))],
        out_specs=pl.BlockSpec((1, tq, C_QW), lambda bi, i: (bi, i, 0)),
        scratch_shapes=[pltpu.VMEM((C_KV_HEADS, C_GROUP * tq, LANES), BF16), state, state],
        compiler_params=_cparams(("parallel", "arbitrary")),
        name="dsa_attention",
    )(proj, proj, proj, bias)


def _mix_ffn_kernel(*refs, n_parts, final_norm):
    x_ref = refs[0]
    parts = refs[1:1 + 2 * n_parts]
    g_ref, wu_ref, wd_ref, gf_ref, o_ref, x1_sc, h_sc, acc_sc = refs[1 + 2 * n_parts:]
    f = pl.program_id(1)

    @pl.when(f == 0)
    def _():
        x1 = x_ref[...]
        for a_ref, w_ref in zip(parts[0::2], parts[1::2]):
            x1 = x1 + jnp.dot(a_ref[...], w_ref[...], preferred_element_type=F32)
        x1_sc[...] = x1
        h_sc[...] = _rms(x1, g_ref[...]).astype(BF16)
        acc_sc[...] = jnp.zeros(acc_sc.shape, F32)

    u = jnp.maximum(jnp.dot(h_sc[...], wu_ref[...], preferred_element_type=F32), 0.0)
    acc_sc[...] += jnp.dot((u * u).astype(BF16), wd_ref[...], preferred_element_type=F32)

    @pl.when(f == pl.num_programs(1) - 1)
    def _():
        y = x1_sc[...] + acc_sc[...]
        o_ref[...] = _rms(y, gf_ref[...]) if final_norm else y


def _mix_ffn(x2, parts, g, wu, wd, g_final, final_norm, tm=1024, tf=512):
    m_rows, d = x2.shape
    tm = min(tm, m_rows)
    dff = wu.shape[1]
    row = lambda cols: pl.BlockSpec((tm, cols), lambda i, f: (i, 0))
    const = lambda shape: pl.BlockSpec(shape, lambda i, f: (0, 0))
    in_specs = [row(d)]
    args = [x2]
    for a, w in parts:
        in_specs += [row(a.shape[1]), const(w.shape)]
        args += [a, w]
    in_specs += [const((1, d)),
                 pl.BlockSpec((d, tf), lambda i, f: (0, f)),
                 pl.BlockSpec((tf, d), lambda i, f: (f, 0)),
                 const((1, d))]
    args += [g.reshape(1, d), wu, wd, g_final.reshape(1, d)]
    return pl.pallas_call(
        functools.partial(_mix_ffn_kernel, n_parts=len(parts), final_norm=final_norm),
        out_shape=jax.ShapeDtypeStruct((m_rows, d), F32),
        grid=(m_rows // tm, dff // tf),
        in_specs=in_specs,
        out_specs=row(d),
        scratch_shapes=[pltpu.VMEM((tm, d), F32), pltpu.VMEM((tm, d), BF16), pltpu.VMEM((tm, d), F32)],
        compiler_params=_cparams(("parallel", "arbitrary")),
        name="mix_ffn",
    )(*args)


Q_SCALE = HEAD_DIM ** -0.5 * math.log2(math.e)


def _even_chunks():
    scale = Q_SCALE
    w = A_W
    return ((0, w, True, scale, 1, 0), (w, w, True, 1.0, 1, w), (2 * w, w, False, 1.0, 1, 2 * w),
            (3 * w, w, True, scale, 0, 0), (4 * w, w, True, 1.0, 0, w), (5 * w, w, False, 1.0, 0, 2 * w))


def _odd_chunks():
    w = 512
    chunks = ((ODD_Q, w, True, Q_SCALE), (ODD_Q + w, w, True, Q_SCALE),
              (ODD_K, w, True, 1.0), (ODD_V, w, False, 1.0),
              (ODD_QI, w, True, IDX_DIM ** -0.5), (ODD_KI, LANES, True, 1.0),
              (ODD_WI, LANES, False, IDX_HEADS ** -0.5))
    return tuple(c + (0, c[0]) for c in chunks)


def _odd_weight(w):
    d = w.shape[0]
    q = w[:, :C_QW]
    k = w[:, C_QW:C_QW + C_KVW].reshape(d, C_KV_HEADS, 1, HEAD_DIM)
    v = w[:, C_QW + C_KVW:C_QW + 2 * C_KVW].reshape(d, C_KV_HEADS, 1, HEAD_DIM)
    dup = lambda t: jnp.broadcast_to(t, (d, C_KV_HEADS, 2, HEAD_DIM)).reshape(d, C_KV_HEADS * LANES)
    o = C_QW + 2 * C_KVW
    qi = w[:, o:o + IDX_HEADS * IDX_DIM]
    ki = w[:, o + IDX_HEADS * IDX_DIM:o + IDX_HEADS * IDX_DIM + IDX_DIM]
    wi = w[:, o + IDX_HEADS * IDX_DIM + IDX_DIM:]
    pad = jnp.zeros((d, LANES - IDX_HEADS), w.dtype)
    return jnp.concatenate([q, dup(k), dup(v), qi, ki, ki, wi, pad], axis=1)


def kernel(x, norm_mix, norm_ffn, w_in_even, w_out_even, lambda_q1, lambda_k1, lambda_q2,
           lambda_k2, diff_subln, w_in_odd, w_out_odd, w_ffn_up, w_ffn_down, norm_final):
    b, s, d = x.shape
    depth = norm_mix.shape[0]
    tables = _rope_tables(s)
    topk = min(TOPK_MAX, s // 4)
    x2 = x.reshape(b * s, d)
    for layer in range(depth):
        if layer % 2 == 0:
            e = layer // 2
            proj_b, proj_a = _norm_proj(x2, norm_mix[layer], w_in_even[e].astype(BF16), tables,
                                        _even_chunks(), s, ((3 * B_W, BF16), (3 * A_W, F32)))
            out_a = _dilated_attention(proj_a.reshape(b, s, 3 * A_W))
            lam_init = 0.8 - 0.6 * math.exp(-0.3 * layer)
            out_b = _diff_attention(proj_b.reshape(b, s, 3 * B_W), lambda_q1[e], lambda_k1[e],
                                    lambda_q2[e], lambda_k2[e], diff_subln[e], lam_init)
            wo = w_out_even[e].astype(BF16)
            parts = [(out_a.reshape(b * s, A_W), wo[:A_W]), (out_b.reshape(b * s, B_W), wo[A_W:])]
        else:
            o = layer // 2
            (proj,) = _norm_proj(x2, norm_mix[layer], _odd_weight(w_in_odd[o]).astype(BF16), tables,
                                 _odd_chunks(), s, ((ODD_N, BF16),))
            proj3 = proj.reshape(b, s, proj.shape[1])
            bias = _dsa_select(proj3, topk)
            out_c = _dsa_attention(proj3, bias)
            parts = [(out_c.reshape(b * s, C_QW), w_out_odd[o].astype(BF16))]
        x2 = _mix_ffn(x2, parts, norm_ffn[layer], w_ffn_up[layer].astype(BF16),
                      w_ffn_down[layer].astype(BF16), norm_final, layer == depth - 1)
    return x2.reshape(b, s, d)
```

```python
import functools
import math

import jax
import jax.numpy as jnp
from jax import lax
from jax.experimental import pallas as pl
from jax.experimental.pallas import tpu as pltpu

D_MODEL = 1024
HEAD_DIM = 64
ROT_DIM = HEAD_DIM // 4
ROPE_THETA = 500000.0
NORM_EPS = 1e-6

A_HEADS = 8
A_PATTERNS = ((128, 1), (512, 4), (2048, 16))
A_W = A_HEADS * HEAD_DIM
B_HEADS = 4
B_VDIM = 2 * HEAD_DIM
B_W = B_HEADS * B_VDIM
C_HEADS = 16
C_KV_HEADS = 4
C_GROUP = C_HEADS // C_KV_HEADS
IDX_HEADS = 8
IDX_DIM = 64
TOPK_MAX = 256
D_FF = 4 * D_MODEL
C_QW = C_HEADS * HEAD_DIM
C_KVW = C_KV_HEADS * HEAD_DIM

LANES = 128
DIL_BLK = 128
NEG = -1e30
VMEM_LIMIT = 56 * 1024 * 1024

BF16 = jnp.bfloat16
F32 = jnp.float32


def _cparams(sem):
    return pltpu.CompilerParams(dimension_semantics=sem, vmem_limit_bytes=VMEM_LIMIT)


def _lo_lanes():
    return lax.broadcasted_iota(jnp.int32, (1, LANES), 1) < HEAD_DIM


def _rep(t, width):
    n = width // LANES
    return t if n == 1 else jnp.concatenate([t] * n, axis=1)


def _dot_nt(a, b):
    return lax.dot_general(a, b, (((1,), (1,)), ((), ())), preferred_element_type=F32)


def _rms(x, g):
    return x * lax.rsqrt(jnp.mean(x * x, axis=-1, keepdims=True) + NORM_EPS) * g


def _softmax_step(s, m_prev, l_prev):
    m_new = jnp.maximum(m_prev, jnp.max(s, axis=-1, keepdims=True))
    alpha = jnp.exp2(m_prev - m_new)
    p = jnp.exp2(s - _rep(m_new, s.shape[1]))
    l_new = alpha * l_prev + jnp.sum(p, axis=-1, keepdims=True)
    return p, m_new, l_new, alpha


def _rope_tables(seq_len):
    pos = jnp.arange(seq_len, dtype=F32)
    inv_freq = jnp.power(ROPE_THETA, -jnp.arange(0, ROT_DIM, 2, dtype=F32) / ROT_DIM)
    ang = pos[:, None] * inv_freq[None, :]
    cos, sin = jnp.cos(ang), jnp.sin(ang)
    half = ROT_DIM // 2
    rest = HEAD_DIM - ROT_DIM
    one = jnp.ones((seq_len, rest), F32)
    z_h = jnp.zeros((seq_len, half), F32)
    z_r = jnp.zeros((seq_len, rest), F32)
    c = jnp.concatenate([cos, cos, one], axis=1)
    sa = jnp.concatenate([z_h, sin, z_r], axis=1)
    sb = jnp.concatenate([-sin, z_h, z_r], axis=1)
    two = lambda t: jnp.concatenate([t, t], axis=1)
    return two(c), two(sa), two(sb)


def _proj_kernel(x_ref, g_ref, w_ref, c_ref, sa_ref, sb_ref, *o_refs, chunks):
    h = _rms(x_ref[...], g_ref[...]).astype(BF16)
    c, sa, sb = c_ref[...], sa_ref[...], sb_ref[...]
    half = ROT_DIM // 2
    for start, width, rope, scale, dest, dstart in chunks:
        acc = jnp.dot(h, w_ref[:, start:start + width], preferred_element_type=F32)
        if rope:
            acc = (acc * _rep(c, width)
                   + pltpu.roll(acc, half, 1) * _rep(sa, width)
                   + pltpu.roll(acc, width - half, 1) * _rep(sb, width))
        if scale != 1.0:
            acc = acc * scale
        o_ref = o_refs[dest]
        o_ref[:, dstart:dstart + width] = acc.astype(o_ref.dtype)


def _norm_proj(x2, g, w, tables, chunks, seq_len, outs, tm=512):
    m_rows, d = x2.shape
    n = w.shape[1]
    nt = seq_len // tm
    tab_spec = pl.BlockSpec((tm, LANES), lambda i: (i % nt, 0))
    return pl.pallas_call(
        functools.partial(_proj_kernel, chunks=chunks),
        out_shape=tuple(jax.ShapeDtypeStruct((m_rows, cols), dt) for cols, dt in outs),
        grid=(m_rows // tm,),
        in_specs=[pl.BlockSpec((tm, d), lambda i: (i, 0)),
                  pl.BlockSpec((1, d), lambda i: (0, 0)),
                  pl.BlockSpec((d, n), lambda i: (0, 0)),
                  tab_spec, tab_spec, tab_spec],
        out_specs=tuple(pl.BlockSpec((tm, cols), lambda i: (i, 0)) for cols, _ in outs),
        compiler_params=_cparams(("parallel",)),
        name="norm_proj",
    )(x2, g.reshape(1, d), w, *tables)


DIL_TILE = DIL_BLK * max(d for _, d in A_PATTERNS)
DIL_BATCH = 4


def _dil_work():
    units = []
    for _, d in A_PATTERNS:
        span = DIL_BLK * d
        for blk in range(DIL_TILE // span):
            for r in range(d):
                off = blk * span + r
                units.append((d, off, (off - span) % DIL_TILE, blk == 0))
    return [units[i:i + DIL_BATCH] for i in range(0, len(units), DIL_BATCH)]


def _dil_kernel(q_ref, kc_ref, kp_ref, vc_ref, vp_ref, o_ref, m_sc, l_sc, acc_sc):
    has_prev = pl.program_id(1) > 0
    blk = DIL_BLK
    lo = _lo_lanes()
    row = lax.broadcasted_iota(jnp.int32, (2 * blk, 2 * blk), 0)
    col = lax.broadcasted_iota(jnp.int32, (2 * blk, 2 * blk), 1)
    rel = jnp.where(row >= blk, row - blk, row) - col + blk
    band = (rel >= 0) & (rel <= blk)
    band_first = band & ((col >= blk) | has_prev)

    def rows(d, off):
        return pl.ds(off, blk, stride=d) if d > 1 else pl.ds(off, blk)

    for units in _dil_work():
        init = units[0][0] == A_PATTERNS[0][1]
        s_parts, v_parts = [], []
        for d, off, poff, prev_tile in units:
            q = q_ref[0, rows(d, off), :].astype(BF16)
            qs = jnp.concatenate([jnp.where(lo, q, 0), jnp.where(lo, 0, q)], axis=0)
            kp, vp = (kp_ref, vp_ref) if prev_tile else (kc_ref, vc_ref)
            kcat = jnp.concatenate([kp[0, rows(d, poff), :], kc_ref[0, rows(d, off), :]], axis=0)
            vcat = jnp.concatenate([vp[0, rows(d, poff), :], vc_ref[0, rows(d, off), :]], axis=0)
            s = _dot_nt(qs, kcat.astype(BF16))
            s_parts.append(jnp.where(band_first if prev_tile else band, s, NEG))
            v_parts.append(vcat.astype(BF16))
        s = jnp.concatenate(s_parts, axis=0)
        m_cur = jnp.max(s, axis=-1, keepdims=True)
        n_rows = s.shape[0]
        if init:
            m_new = jnp.broadcast_to(m_cur, (n_rows, LANES))
            p = jnp.exp2(s - m_cur)
            l_new = jnp.broadcast_to(jnp.sum(p, axis=-1, keepdims=True), (n_rows, LANES))
        else:
            state = lambda sc: jnp.concatenate(
                [sc[h, rows(d, off), :] for d, off, _, _ in units for h in range(2)], axis=0)
            m_prev = state(m_sc)
            m_new = jnp.maximum(m_prev, m_cur)
            alpha = jnp.exp2(m_prev - m_new)
            p = jnp.exp2(s - _rep(m_new, 2 * blk))
            l_new = alpha * state(l_sc) + jnp.sum(p, axis=-1, keepdims=True)
            acc_prev = state(acc_sc)
        p = p.astype(BF16)
        for u, (d, off, _, _) in enumerate(units):
            r0 = u * 2 * blk
            pv = jnp.dot(p[r0:r0 + 2 * blk], v_parts[u], preferred_element_type=F32)
            if not init:
                pv = alpha[r0:r0 + 2 * blk] * acc_prev[r0:r0 + 2 * blk] + pv
            for h in range(2):
                sl = slice(r0 + h * blk, r0 + (h + 1) * blk)
                m_sc[h, rows(d, off), :] = m_new[sl]
                l_sc[h, rows(d, off), :] = l_new[sl]
                acc_sc[h, rows(d, off), :] = pv[h * blk:(h + 1) * blk]

    o_ref[0] = jnp.where(lo, acc_sc[0] / l_sc[0], acc_sc[1] / l_sc[1]).astype(o_ref.dtype)


def _dilated_attention(qkv):
    b, s, _ = qkv.shape
    n_pairs = A_W // LANES
    blk = (1, DIL_TILE, LANES)
    cur = lambda c: pl.BlockSpec(blk, lambda bi, n, p: (bi, n, c * n_pairs + p))
    prev = lambda c: pl.BlockSpec(blk, lambda bi, n, p: (bi, jnp.maximum(n - 1, 0), c * n_pairs + p))
    state = pltpu.VMEM((2, DIL_TILE, LANES), F32)
    return pl.pallas_call(
        _dil_kernel,
        out_shape=jax.ShapeDtypeStruct((b, s, A_W), BF16),
        grid=(b, s // DIL_TILE, n_pairs),
        in_specs=[cur(0), cur(1), prev(1), cur(2), prev(2)],
        out_specs=pl.BlockSpec(blk, lambda bi, n, p: (bi, n, p)),
        scratch_shapes=[state, state, state],
        compiler_params=_cparams(("parallel", "parallel", "arbitrary")),
        name="dilated_attention",
    )(qkv, qkv, qkv, qkv, qkv)


def _diff_kernel(q_ref, k_ref, v_ref, lq1, lk1, lq2, lk2, g_ref, o_ref,
                 qs_sc, m_sc, l_sc, acc_sc, *, tq, lam_init):
    i = pl.program_id(2)
    lo = _lo_lanes()
    q = q_ref[0]
    qs_sc[...] = jnp.concatenate([jnp.where(lo, q, 0), jnp.where(lo, 0, q)], axis=0)
    m_sc[...] = jnp.full(m_sc.shape, NEG, F32)
    l_sc[...] = jnp.zeros(l_sc.shape, F32)
    acc_sc[...] = jnp.zeros(acc_sc.shape, F32)

    def step(j, masked):
        start = pl.multiple_of(j * tq, tq)
        k = k_ref[0, pl.ds(start, tq), :]
        v = v_ref[0, pl.ds(start, tq), :]
        s = _dot_nt(qs_sc[...], k)
        if masked:
            row = lax.broadcasted_iota(jnp.int32, (2 * tq, tq), 0)
            col = lax.broadcasted_iota(jnp.int32, (2 * tq, tq), 1)
            s = jnp.where(col <= jnp.where(row >= tq, row - tq, row), s, NEG)
        p, m_new, l_new, alpha = _softmax_step(s, m_sc[...], l_sc[...])
        m_sc[...] = m_new
        l_sc[...] = l_new
        acc_sc[...] = alpha * acc_sc[...] + jnp.dot(p.astype(BF16), v, preferred_element_type=F32)

    def body(j, carry):
        step(j, False)
        return carry

    lax.fori_loop(0, i, body, 0)
    step(i, True)

    lam = (jnp.exp(jnp.sum(lq1[...] * lk1[...], axis=-1, keepdims=True))
           - jnp.exp(jnp.sum(lq2[...] * lk2[...], axis=-1, keepdims=True)) + lam_init)
    o = acc_sc[...] / l_sc[...]
    o = o[:tq] - lam * o[tq:]
    o_ref[0] = (_rms(o, g_ref[...]) * (1.0 - lam_init)).astype(o_ref.dtype)


def _diff_attention(proj, lq1, lk1, lq2, lk2, subln, lam_init, tq=512):
    b, s, _ = proj.shape
    qb, kb, vb = 0, B_W // LANES, 2 * B_W // LANES
    vec = pl.BlockSpec((1, HEAD_DIM), lambda bi, h, i: (0, 0))
    return pl.pallas_call(
        functools.partial(_diff_kernel, tq=tq, lam_init=lam_init),
        out_shape=jax.ShapeDtypeStruct((b, s, B_W), BF16),
        grid=(b, B_HEADS, s // tq),
        in_specs=[pl.BlockSpec((1, tq, LANES), lambda bi, h, i: (bi, i, qb + h)),
                  pl.BlockSpec((1, s, LANES), lambda bi, h, i: (bi, 0, kb + h)),
                  pl.BlockSpec((1, s, LANES), lambda bi, h, i: (bi, 0, vb + h)),
                  vec, vec, vec, vec,
                  pl.BlockSpec((1, B_VDIM), lambda bi, h, i: (0, 0))],
        out_specs=pl.BlockSpec((1, tq, LANES), lambda bi, h, i: (bi, i, h)),
        scratch_shapes=[pltpu.VMEM((2 * tq, LANES), BF16)] + [pltpu.VMEM((2 * tq, LANES), F32)] * 3,
        compiler_params=_cparams(("parallel", "parallel", "arbitrary")),
        name="diff_attention",
    )(proj, proj, proj, lq1.reshape(1, -1), lk1.reshape(1, -1), lq2.reshape(1, -1),
      lk2.reshape(1, -1), subln.reshape(1, -1))


ODD_Q = 0
ODD_K = C_QW
ODD_V = ODD_K + C_KV_HEADS * LANES
ODD_QI = ODD_V + C_KV_HEADS * LANES
ODD_KI = ODD_QI + IDX_HEADS * IDX_DIM
ODD_WI = ODD_KI + LANES
ODD_N = ODD_WI + LANES


def _stack_heads(x, lo):
    parts = []
    for pair in range(x.shape[1] // LANES):
        xp = x[:, pair * LANES:(pair + 1) * LANES]
        parts += [jnp.where(lo, xp, 0), jnp.where(lo, 0, xp)]
    return jnp.concatenate(parts, axis=0)


WORD_BITS = 32
PLANE_ROWS = 8 * WORD_BITS
_SWAPS = ((16, 0x0000FFFF), (8, 0x00FF00FF), (4, 0x0F0F0F0F), (2, 0x33333333), (1, 0x55555555))


def _transpose32(words):
    a = list(words)
    for j, m in _SWAPS:
        for k in range(WORD_BITS):
            if k & j == 0:
                t = (lax.shift_right_logical(a[k], jnp.int32(j)) ^ a[k + j]) & jnp.int32(m)
                a[k + j] = a[k + j] ^ t
                a[k] = a[k] ^ lax.shift_left(t, jnp.int32(j))
    return a


def _index_kernel(qi_ref, ki_ref, wi_ref, bias_ref, qs_sc, p_sc, c_sc, g_sc, t_sc,
                  *, tq, tk, nkc, topk, idx_bits):
    i = pl.program_id(1)
    nv = (i * tq) // tk + 1
    ppc = tk // PLANE_ROWS
    kf = float(topk)
    krow = lax.broadcasted_iota(jnp.int32, (tk, tq), 0)
    qpos = i * tq + lax.broadcasted_iota(jnp.int32, (tk, tq), 1)
    srow = lax.broadcasted_iota(jnp.int32, (8, tq), 0)
    qs_sc[...] = _stack_heads(qi_ref[0], _lo_lanes())
    w_t = wi_ref[0].astype(F32).T

    def score_body(j, carry):
        start = pl.multiple_of(j * tk, tk)
        s = _dot_nt(ki_ref[0, pl.ds(start, tk), :], qs_sc[...])
        score = jnp.zeros((tk, tq), F32)
        for h in range(IDX_HEADS):
            score = score + jnp.maximum(s[:, h * tq:(h + 1) * tq], 0.0) * w_t[h:h + 1, :]
        score = jnp.where(start + krow <= qpos, score, -jnp.inf)
        bits = lax.bitcast_convert_type(score, jnp.int32)
        key = bits ^ (lax.shift_right_arithmetic(bits, jnp.int32(31)) | jnp.int32(-2 ** 31))
        for t in range(ppc):
            r0 = t * PLANE_ROWS
            planes = _transpose32([key[r0 + v * 8:r0 + v * 8 + 8, :] for v in range(WORD_BITS)])
            for b in range(WORD_BITS):
                p_sc[b, j * ppc + t] = planes[b]
        return carry

    lax.fori_loop(0, nv, score_body, 0)

    def each_tile(fn, init):
        def body(j, carry):
            for t in range(ppc):
                carry = fn(j * ppc + t, carry)
            return carry
        return lax.fori_loop(0, nv, body, init)

    def lane_count(word_fn):
        acc = each_tile(lambda pv, a: a + lax.population_count(word_fn(pv)), jnp.zeros((8, tq), jnp.int32))
        return jnp.sum(acc.astype(F32), axis=0, keepdims=True)

    c_sc[...] = jnp.full(c_sc.shape, -1, jnp.int32)
    g_sc[...] = jnp.zeros(g_sc.shape, jnp.int32)

    def bit_body(t, above):
        b = WORD_BITS - 1 - t
        ones = lane_count(lambda pv: c_sc[pv] & p_sc[b, pv])
        take = above + ones >= kf
        takem = jnp.where(take, -1, 0).astype(jnp.int32)

        def update(pv, carry):
            c = c_sc[pv]
            hi = c & p_sc[b, pv]
            c_sc[pv] = (hi & takem) | ((c ^ hi) & ~takem)
            g_sc[pv] = g_sc[pv] | (hi & ~takem)
            return carry

        each_tile(update, 0)
        return jnp.where(take, above, above + ones)

    above = lax.fori_loop(0, WORD_BITS, bit_body, jnp.zeros((1, tq), F32))

    def below(pv, c):
        nb = jnp.clip(lax.shift_right_arithmetic(c - pv * PLANE_ROWS - srow + 7, jnp.int32(3)), 0, WORD_BITS)
        return jnp.where(nb >= WORD_BITS, -1, lax.shift_left(jnp.int32(1), nb) - 1)

    t_sc[...] = jnp.full(t_sc.shape, -1, jnp.int32)
    need = kf - above
    n_tie = lane_count(lambda pv: c_sc[pv])

    @pl.when(jnp.max(n_tie - need) > 0.0)
    def _():
        def idx_body(t, cur):
            cand = cur | lax.shift_left(jnp.int32(1), idx_bits - 1 - t)
            cnt = lane_count(lambda pv: c_sc[pv] & below(pv, cand))
            return jnp.where(cnt < need, cand, cur)

        cut = lax.fori_loop(0, idx_bits, idx_body, jnp.zeros((1, tq), jnp.int32))

        def keep(pv, carry):
            t_sc[pv] = below(pv, cut + 1)
            return carry

        each_tile(keep, 0)

    for j in range(nkc):
        @pl.when(j < nv)
        def _():
            pieces = []
            for t in range(ppc):
                pv = j * ppc + t
                sel = g_sc[pv] | (c_sc[pv] & t_sc[pv])
                pieces += [lax.shift_right_logical(sel, jnp.int32(v)) & 1 for v in range(WORD_BITS)]
            sel = jnp.concatenate(pieces, axis=0)
            ok = (sel != 0) & (j * tk + krow <= qpos)
            bias_ref[0, j] = jnp.where(ok, 0.0, NEG).T.astype(bias_ref.dtype)

        @pl.when(j >= nv)
        def _():
            bias_ref[0, j] = jnp.full((tq, tk), NEG, bias_ref.dtype)


def _dsa_select(proj, topk, tq=256, tk=512):
    b, s, _ = proj.shape
    nkc = s // tk
    idx_bits = max(1, (s - 1).bit_length())
    qiw = IDX_HEADS * IDX_DIM
    words = pltpu.VMEM((s // PLANE_ROWS, 8, tq), jnp.int32)
    return pl.pallas_call(
        functools.partial(_index_kernel, tq=tq, tk=tk, nkc=nkc, topk=topk, idx_bits=idx_bits),
        out_shape=jax.ShapeDtypeStruct((b, nkc, s, tk), BF16),
        grid=(b, s // tq),
        in_specs=[pl.BlockSpec((1, tq, qiw), lambda bi, i: (bi, i, ODD_QI // qiw)),
                  pl.BlockSpec((1, s, LANES), lambda bi, i: (bi, 0, ODD_KI // LANES)),
                  pl.BlockSpec((1, tq, LANES), lambda bi, i: (bi, i, ODD_WI // LANES))],
        out_specs=pl.BlockSpec((1, nkc, tq, tk), lambda bi, i: (bi, 0, i, 0)),
        scratch_shapes=[pltpu.VMEM((IDX_HEADS * tq, LANES), BF16),
                        pltpu.VMEM((WORD_BITS, s // PLANE_ROWS, 8, tq), jnp.int32),
                        words, words, words],
        compiler_params=_cparams(("parallel", "arbitrary")),
        name="dsa_select",
    )(proj, proj, proj)


def _dsa_kernel(q_ref, k_ref, v_ref, bias_ref, o_ref, qs_sc, m_sc, acc_sc, *, tq, tk):
    i = pl.program_id(1)
    nv = (i * tq) // tk + 1
    lo = _lo_lanes()
    lane = lax.broadcasted_iota(jnp.int32, (1, LANES), 1)
    one_col = jnp.where(lane == HEAD_DIM, 1, 0).astype(BF16)
    gw = C_GROUP * HEAD_DIM
    for g in range(C_KV_HEADS):
        qs_sc[g] = _stack_heads(q_ref[0, :, g * gw:(g + 1) * gw], lo)
    m_sc[...] = jnp.full(m_sc.shape, NEG, F32)
    acc_sc[...] = jnp.zeros(acc_sc.shape, F32)

    def body(j, carry):
        start = pl.multiple_of(j * tk, tk)
        bias = bias_ref[0, j].astype(F32)
        bias = jnp.concatenate([bias] * C_GROUP, axis=0)
        for g in range(C_KV_HEADS):
            k = k_ref[0, pl.ds(start, tk), g * LANES:(g + 1) * LANES]
            v = jnp.where(lo, v_ref[0, pl.ds(start, tk), g * LANES:(g + 1) * LANES], one_col)
            s = _dot_nt(qs_sc[g], k) + bias
            m_prev = m_sc[g]
            m_new = jnp.maximum(m_prev, jnp.max(s, axis=-1, keepdims=True))
            p = jnp.exp2(s - _rep(m_new, tk))
            m_sc[g] = m_new
            acc_sc[g] = (jnp.exp2(m_prev - m_new) * acc_sc[g]
                         + jnp.dot(p.astype(BF16), v, preferred_element_type=F32))
        return carry

    lax.fori_loop(0, nv, body, 0)
    for g in range(C_KV_HEADS):
        acc = acc_sc[g]
        o = acc / acc[:, HEAD_DIM:HEAD_DIM + 1]
        for pair in range(gw // LANES):
            r = 2 * pair * tq
            col = g * gw + pair * LANES
            odd = pltpu.roll(o[r + tq:r + 2 * tq], HEAD_DIM, 1)
            o_ref[0, :, col:col + LANES] = jnp.where(lo, o[r:r + tq], odd).astype(o_ref.dtype)


def _dsa_attention(proj, bias, tq=512, tk=512):
    b, s, _ = proj.shape
    nkc = s // tk
    kvw = C_KV_HEADS * LANES
    state = pltpu.VMEM((C_KV_HEADS, C_GROUP * tq, LANES), F32)
    return pl.pallas_call(
        functools.partial(_dsa_kernel, tq=tq, tk=tk),
        out_shape=jax.ShapeDtypeStruct((b, s, C_QW), BF16),
        grid=(b, s // tq),
        in_specs=[pl.BlockSpec((1, tq, C_QW), lambda bi, i: (bi, i, ODD_Q // C_QW)),
                  pl.BlockSpec((1, s, kvw), lambda bi, i: (bi, 0, ODD_K // kvw)),
                  pl.BlockSpec((1, s, kvw), lambda bi, i: (bi, 0, ODD_V // kvw)),
                  pl.BlockSpec((1, nkc, tq, tk), lambda bi, i: (bi, 0, i, 0))],
        out_specs=pl.BlockSpec((1, tq, C_QW), lambda bi, i: (bi, i, 0)),
        scratch_shapes=[pltpu.VMEM((C_KV_HEADS, C_GROUP * tq, LANES), BF16), state, state],
        compiler_params=_cparams(("parallel", "arbitrary")),
        name="dsa_attention",
    )(proj, proj, proj, bias)


def _mix_ffn_kernel(*refs, n_parts, final_norm):
    x_ref = refs[0]
    parts = refs[1:1 + 2 * n_parts]
    g_ref, wu_ref, wd_ref, gf_ref, o_ref, x1_sc, h_sc, acc_sc = refs[1 + 2 * n_parts:]
    f = pl.program_id(1)

    @pl.when(f == 0)
    def _():
        x1 = x_ref[...]
        for a_ref, w_ref in zip(parts[0::2], parts[1::2]):
            x1 = x1 + jnp.dot(a_ref[...], w_ref[...], preferred_element_type=F32)
        x1_sc[...] = x1
        h_sc[...] = _rms(x1, g_ref[...]).astype(BF16)
        acc_sc[...] = jnp.zeros(acc_sc.shape, F32)

    u = jnp.maximum(jnp.dot(h_sc[...], wu_ref[...], preferred_element_type=F32), 0.0)
    acc_sc[...] += jnp.dot((u * u).astype(BF16), wd_ref[...], preferred_element_type=F32)

    @pl.when(f == pl.num_programs(1) - 1)
    def _():
        y = x1_sc[...] + acc_sc[...]
        o_ref[...] = _rms(y, gf_ref[...]) if final_norm else y


def _mix_ffn(x2, parts, g, wu, wd, g_final, final_norm, tm=1024, tf=512):
    m_rows, d = x2.shape
    tm = min(tm, m_rows)
    dff = wu.shape[1]
    row = lambda cols: pl.BlockSpec((tm, cols), lambda i, f: (i, 0))
    const = lambda shape: pl.BlockSpec(shape, lambda i, f: (0, 0))
    in_specs = [row(d)]
    args = [x2]
    for a, w in parts:
        in_specs += [row(a.shape[1]), const(w.shape)]
        args += [a, w]
    in_specs += [const((1, d)),
                 pl.BlockSpec((d, tf), lambda i, f: (0, f)),
                 pl.BlockSpec((tf, d), lambda i, f: (f, 0)),
                 const((1, d))]
    args += [g.reshape(1, d), wu, wd, g_final.reshape(1, d)]
    return pl.pallas_call(
        functools.partial(_mix_ffn_kernel, n_parts=len(parts), final_norm=final_norm),
        out_shape=jax.ShapeDtypeStruct((m_rows, d), F32),
        grid=(m_rows // tm, dff // tf),
        in_specs=in_specs,
        out_specs=row(d),
        scratch_shapes=[pltpu.VMEM((tm, d), F32), pltpu.VMEM((tm, d), BF16), pltpu.VMEM((tm, d), F32)],
        compiler_params=_cparams(("parallel", "arbitrary")),
        name="mix_ffn",
    )(*args)


Q_SCALE = HEAD_DIM ** -0.5 * math.log2(math.e)


def _even_chunks():
    scale = Q_SCALE
    w = A_W
    return ((0, w, True, scale, 1, 0), (w, w, True, 1.0, 1, w), (2 * w, w, False, 1.0, 1, 2 * w),
            (3 * w, w, True, scale, 0, 0), (4 * w, w, True, 1.0, 0, w), (5 * w, w, False, 1.0, 0, 2 * w))


def _odd_chunks():
    w = 512
    chunks = ((ODD_Q, w, True, Q_SCALE), (ODD_Q + w, w, True, Q_SCALE),
              (ODD_K, w, True, 1.0), (ODD_V, w, False, 1.0),
              (ODD_QI, w, True, IDX_DIM ** -0.5), (ODD_KI, LANES, True, 1.0),
              (ODD_WI, LANES, False, IDX_HEADS ** -0.5))
    return tuple(c + (0, c[0]) for c in chunks)


def _odd_weight(w):
    d = w.shape[0]
    q = w[:, :C_QW]
    k = w[:, C_QW:C_QW + C_KVW].reshape(d, C_KV_HEADS, 1, HEAD_DIM)
    v = w[:, C_QW + C_KVW:C_QW + 2 * C_KVW].reshape(d, C_KV_HEADS, 1, HEAD_DIM)
    dup = lambda t: jnp.broadcast_to(t, (d, C_KV_HEADS, 2, HEAD_DIM)).reshape(d, C_KV_HEADS * LANES)
    o = C_QW + 2 * C_KVW
    qi = w[:, o:o + IDX_HEADS * IDX_DIM]
    ki = w[:, o + IDX_HEADS * IDX_DIM:o + IDX_HEADS * IDX_DIM + IDX_DIM]
    wi = w[:, o + IDX_HEADS * IDX_DIM + IDX_DIM:]
    pad = jnp.zeros((d, LANES - IDX_HEADS), w.dtype)
    return jnp.concatenate([q, dup(k), dup(v), qi, ki, ki, wi, pad], axis=1)


def kernel(x, norm_mix, norm_ffn, w_in_even, w_out_even, lambda_q1, lambda_k1, lambda_q2,
           lambda_k2, diff_subln, w_in_odd, w_out_odd, w_ffn_up, w_ffn_down, norm_final):
    b, s, d = x.shape
    depth = norm_mix.shape[0]
    tables = _rope_tables(s)
    topk = min(TOPK_MAX, s // 4)
    x2 = x.reshape(b * s, d)
    for layer in range(depth):
        if layer % 2 == 0:
            e = layer // 2
            proj_b, proj_a = _norm_proj(x2, norm_mix[layer], w_in_even[e].astype(BF16), tables,
                                        _even_chunks(), s, ((3 * B_W, BF16), (3 * A_W, F32)))
            out_a = _dilated_attention(proj_a.reshape(b, s, 3 * A_W))
            lam_init = 0.8 - 0.6 * math.exp(-0.3 * layer)
            out_b = _diff_attention(proj_b.reshape(b, s, 3 * B_W), lambda_q1[e], lambda_k1[e],
                                    lambda_q2[e], lambda_k2[e], diff_subln[e], lam_init)
            wo = w_out_even[e].astype(BF16)
            parts = [(out_a.reshape(b * s, A_W), wo[:A_W]), (out_b.reshape(b * s, B_W), wo[A_W:])]
        else:
            o = layer // 2
            (proj,) = _norm_proj(x2, norm_mix[layer], _odd_weight(w_in_odd[o]).astype(BF16), tables,
                                 _odd_chunks(), s, ((ODD_N, BF16),))
            proj3 = proj.reshape(b, s, proj.shape[1])
            bias = _dsa_select(proj3, topk)
            out_c = _dsa_attention(proj3, bias)
            parts = [(out_c.reshape(b * s, C_QW), w_out_odd[o].astype(BF16))]
        x2 = _mix_ffn(x2, parts, norm_ffn[layer], w_ffn_up[layer].astype(BF16),
                      w_ffn_down[layer].astype(BF16), norm_final, layer == depth - 1)
    return x2.reshape(b, s, d)
```

```python
import functools
import math

import jax
import jax.numpy as jnp
from jax import lax
from jax.experimental import pallas as pl
from jax.experimental.pallas import tpu as pltpu

D_MODEL = 1024
HEAD_DIM = 64
ROT_DIM = HEAD_DIM // 4
ROPE_THETA = 500000.0
NORM_EPS = 1e-6

A_HEADS = 8
A_PATTERNS = ((128, 1), (512, 4), (2048, 16))
A_W = A_HEADS * HEAD_DIM
B_HEADS = 4
B_VDIM = 2 * HEAD_DIM
B_W = B_HEADS * B_VDIM
C_HEADS = 16
C_KV_HEADS = 4
C_GROUP = C_HEADS // C_KV_HEADS
IDX_HEADS = 8
IDX_DIM = 64
TOPK_MAX = 256
D_FF = 4 * D_MODEL
C_QW = C_HEADS * HEAD_DIM
C_KVW = C_KV_HEADS * HEAD_DIM

LANES = 128
DIL_BLK = 128
NEG = -1e30
VMEM_LIMIT = 56 * 1024 * 1024

BF16 = jnp.bfloat16
F32 = jnp.float32


def _cparams(sem):
    return pltpu.CompilerParams(dimension_semantics=sem, vmem_limit_bytes=VMEM_LIMIT)


def _lo_lanes():
    return lax.broadcasted_iota(jnp.int32, (1, LANES), 1) < HEAD_DIM


def _rep(t, width):
    n = width // LANES
    return t if n == 1 else jnp.concatenate([t] * n, axis=1)


def _dot_nt(a, b):
    return lax.dot_general(a, b, (((1,), (1,)), ((), ())), preferred_element_type=F32)


def _rms(x, g):
    return x * lax.rsqrt(jnp.mean(x * x, axis=-1, keepdims=True) + NORM_EPS) * g


def _softmax_step(s, m_prev, l_prev):
    m_new = jnp.maximum(m_prev, jnp.max(s, axis=-1, keepdims=True))
    alpha = jnp.exp2(m_prev - m_new)
    p = jnp.exp2(s - _rep(m_new, s.shape[1]))
    l_new = alpha * l_prev + jnp.sum(p, axis=-1, keepdims=True)
    return p, m_new, l_new, alpha


def _rope_tables(seq_len):
    pos = jnp.arange(seq_len, dtype=F32)
    inv_freq = jnp.power(ROPE_THETA, -jnp.arange(0, ROT_DIM, 2, dtype=F32) / ROT_DIM)
    ang = pos[:, None] * inv_freq[None, :]
    cos, sin = jnp.cos(ang), jnp.sin(ang)
    half = ROT_DIM // 2
    rest = HEAD_DIM - ROT_DIM
    one = jnp.ones((seq_len, rest), F32)
    z_h = jnp.zeros((seq_len, half), F32)
    z_r = jnp.zeros((seq_len, rest), F32)
    c = jnp.concatenate([cos, cos, one], axis=1)
    sa = jnp.concatenate([z_h, sin, z_r], axis=1)
    sb = jnp.concatenate([-sin, z_h, z_r], axis=1)
    two = lambda t: jnp.concatenate([t, t], axis=1)
    return two(c), two(sa), two(sb)


def _proj_kernel(x_ref, g_ref, w_ref, c_ref, sa_ref, sb_ref, *o_refs, chunks):
    h = _rms(x_ref[...], g_ref[...]).astype(BF16)
    c, sa, sb = c_ref[...], sa_ref[...], sb_ref[...]
    half = ROT_DIM // 2
    for start, width, rope, scale, dest, dstart in chunks:
        acc = jnp.dot(h, w_ref[:, start:start + width], preferred_element_type=F32)
        if rope:
            acc = (acc * _rep(c, width)
                   + pltpu.roll(acc, half, 1) * _rep(sa, width)
                   + pltpu.roll(acc, width - half, 1) * _rep(sb, width))
        if scale != 1.0:
            acc = acc * scale
        o_ref = o_refs[dest]
        o_ref[:, dstart:dstart + width] = acc.astype(o_ref.dtype)


def _norm_proj(x2, g, w, tables, chunks, seq_len, outs, tm=512):
    m_rows, d = x2.shape
    n = w.shape[1]
    nt = seq_len // tm
    tab_spec = pl.BlockSpec((tm, LANES), lambda i: (i % nt, 0))
    return pl.pallas_call(
        functools.partial(_proj_kernel, chunks=chunks),
        out_shape=tuple(jax.ShapeDtypeStruct((m_rows, cols), dt) for cols, dt in outs),
        grid=(m_rows // tm,),
        in_specs=[pl.BlockSpec((tm, d), lambda i: (i, 0)),
                  pl.BlockSpec((1, d), lambda i: (0, 0)),
                  pl.BlockSpec((d, n), lambda i: (0, 0)),
                  tab_spec, tab_spec, tab_spec],
        out_specs=tuple(pl.BlockSpec((tm, cols), lambda i: (i, 0)) for cols, _ in outs),
        compiler_params=_cparams(("parallel",)),
        name="norm_proj",
    )(x2, g.reshape(1, d), w, *tables)


DIL_TILE = DIL_BLK * max(d for _, d in A_PATTERNS)
DIL_BATCH = 4


def _dil_work():
    units = []
    for _, d in A_PATTERNS:
        span = DIL_BLK * d
        for blk in range(DIL_TILE // span):
            for r in range(d):
                off = blk * span + r
                units.append((d, off, (off - span) % DIL_TILE, blk == 0))
    return [units[i:i + DIL_BATCH] for i in range(0, len(units), DIL_BATCH)]


def _dil_kernel(q_ref, kc_ref, kp_ref, vc_ref, vp_ref, o_ref, m_sc, l_sc, acc_sc):
    has_prev = pl.program_id(1) > 0
    blk = DIL_BLK
    lo = _lo_lanes()
    row = lax.broadcasted_iota(jnp.int32, (2 * blk, 2 * blk), 0)
    col = lax.broadcasted_iota(jnp.int32, (2 * blk, 2 * blk), 1)
    rel = jnp.where(row >= blk, row - blk, row) - col + blk
    band = (rel >= 0) & (rel <= blk)
    band_first = band & ((col >= blk) | has_prev)

    def rows(d, off):
        return pl.ds(off, blk, stride=d) if d > 1 else pl.ds(off, blk)

    for units in _dil_work():
        init = units[0][0] == A_PATTERNS[0][1]
        s_parts, v_parts = [], []
        for d, off, poff, prev_tile in units:
            q = q_ref[0, rows(d, off), :].astype(BF16)
            qs = jnp.concatenate([jnp.where(lo, q, 0), jnp.where(lo, 0, q)], axis=0)
            kp, vp = (kp_ref, vp_ref) if prev_tile else (kc_ref, vc_ref)
            kcat = jnp.concatenate([kp[0, rows(d, poff), :], kc_ref[0, rows(d, off), :]], axis=0)
            vcat = jnp.concatenate([vp[0, rows(d, poff), :], vc_ref[0, rows(d, off), :]], axis=0)
            s = _dot_nt(qs, kcat.astype(BF16))
            s_parts.append(jnp.where(band_first if prev_tile else band, s, NEG))
            v_parts.append(vcat.astype(BF16))
        s = jnp.concatenate(s_parts, axis=0)
        m_cur = jnp.max(s, axis=-1, keepdims=True)
        n_rows = s.shape[0]
        if init:
            m_new = jnp.broadcast_to(m_cur, (n_rows, LANES))
            p = jnp.exp2(s - m_cur)
            l_new = jnp.broadcast_to(jnp.sum(p, axis=-1, keepdims=True), (n_rows, LANES))
        else:
            state = lambda sc: jnp.concatenate(
                [sc[h, rows(d, off), :] for d, off, _, _ in units for h in range(2)], axis=0)
            m_prev = state(m_sc)
            m_new = jnp.maximum(m_prev, m_cur)
            alpha = jnp.exp2(m_prev - m_new)
            p = jnp.exp2(s - _rep(m_new, 2 * blk))
            l_new = alpha * state(l_sc) + jnp.sum(p, axis=-1, keepdims=True)
            acc_prev = state(acc_sc)
        p = p.astype(BF16)
        for u, (d, off, _, _) in enumerate(units):
            r0 = u * 2 * blk
            pv = jnp.dot(p[r0:r0 + 2 * blk], v_parts[u], preferred_element_type=F32)
            if not init:
                pv = alpha[r0:r0 + 2 * blk] * acc_prev[r0:r0 + 2 * blk] + pv
            for h in range(2):
                sl = slice(r0 + h * blk, r0 + (h + 1) * blk)
                m_sc[h, rows(d, off), :] = m_new[sl]
                l_sc[h, rows(d, off), :] = l_new[sl]
                acc_sc[h, rows(d, off), :] = pv[h * blk:(h + 1) * blk]

    o_ref[0] = jnp.where(lo, acc_sc[0] / l_sc[0], acc_sc[1] / l_sc[1]).astype(o_ref.dtype)


def _dilated_attention(qkv):
    b, s, _ = qkv.shape
    n_pairs = A_W // LANES
    blk = (1, DIL_TILE, LANES)
    cur = lambda c: pl.BlockSpec(blk, lambda bi, n, p: (bi, n, c * n_pairs + p))
    prev = lambda c: pl.BlockSpec(blk, lambda bi, n, p: (bi, jnp.maximum(n - 1, 0), c * n_pairs + p))
    state = pltpu.VMEM((2, DIL_TILE, LANES), F32)
    return pl.pallas_call(
        _dil_kernel,
        out_shape=jax.ShapeDtypeStruct((b, s, A_W), BF16),
        grid=(b, s // DIL_TILE, n_pairs),
        in_specs=[cur(0), cur(1), prev(1), cur(2), prev(2)],
        out_specs=pl.BlockSpec(blk, lambda bi, n, p: (bi, n, p)),
        scratch_shapes=[state, state, state],
        compiler_params=_cparams(("parallel", "parallel", "arbitrary")),
        name="dilated_attention",
    )(qkv, qkv, qkv, qkv, qkv)


def _diff_kernel(q_ref, k_ref, v_ref, lq1, lk1, lq2, lk2, g_ref, o_ref,
                 qs_sc, m_sc, l_sc, acc_sc, *, tq, lam_init):
    i = pl.program_id(1)
    lo = _lo_lanes()
    for h in range(B_HEADS):
        q = q_ref[0, :, h * LANES:(h + 1) * LANES]
        qs_sc[h] = jnp.concatenate([jnp.where(lo, q, 0), jnp.where(lo, 0, q)], axis=0)
    m_sc[...] = jnp.full(m_sc.shape, NEG, F32)
    l_sc[...] = jnp.zeros(l_sc.shape, F32)
    acc_sc[...] = jnp.zeros(acc_sc.shape, F32)

    def step(j, masked):
        start = pl.multiple_of(j * tq, tq)
        for h in range(B_HEADS):
            k = k_ref[0, pl.ds(start, tq), h * LANES:(h + 1) * LANES]
            v = v_ref[0, pl.ds(start, tq), h * LANES:(h + 1) * LANES]
            s = _dot_nt(qs_sc[h], k)
            if masked:
                row = lax.broadcasted_iota(jnp.int32, (2 * tq, tq), 0)
                col = lax.broadcasted_iota(jnp.int32, (2 * tq, tq), 1)
                s = jnp.where(col <= jnp.where(row >= tq, row - tq, row), s, NEG)
            p, m_new, l_new, alpha = _softmax_step(s, m_sc[h], l_sc[h])
            m_sc[h] = m_new
            l_sc[h] = l_new
            acc_sc[h] = alpha * acc_sc[h] + jnp.dot(p.astype(BF16), v, preferred_element_type=F32)

    def body(j, carry):
        step(j, False)
        return carry

    lax.fori_loop(0, i, body, 0)
    step(i, True)

    lam = (jnp.exp(jnp.sum(lq1[...] * lk1[...], axis=-1, keepdims=True))
           - jnp.exp(jnp.sum(lq2[...] * lk2[...], axis=-1, keepdims=True)) + lam_init)
    for h in range(B_HEADS):
        o = acc_sc[h] / l_sc[h]
        o = o[:tq] - lam * o[tq:]
        o_ref[0, :, h * LANES:(h + 1) * LANES] = (_rms(o, g_ref[...]) * (1.0 - lam_init)).astype(o_ref.dtype)


def _diff_attention(proj, lq1, lk1, lq2, lk2, subln, lam_init, tq=512):
    b, s, _ = proj.shape
    vec = pl.BlockSpec((1, HEAD_DIM), lambda bi, i: (0, 0))
    state = pltpu.VMEM((B_HEADS, 2 * tq, LANES), F32)
    return pl.pallas_call(
        functools.partial(_diff_kernel, tq=tq, lam_init=lam_init),
        out_shape=jax.ShapeDtypeStruct((b, s, B_W), BF16),
        grid=(b, s // tq),
        in_specs=[pl.BlockSpec((1, tq, B_W), lambda bi, i: (bi, i, 0)),
                  pl.BlockSpec((1, s, B_W), lambda bi, i: (bi, 0, 1)),
                  pl.BlockSpec((1, s, B_W), lambda bi, i: (bi, 0, 2)),
                  vec, vec, vec, vec,
                  pl.BlockSpec((1, B_VDIM), lambda bi, i: (0, 0))],
        out_specs=pl.BlockSpec((1, tq, B_W), lambda bi, i: (bi, i, 0)),
        scratch_shapes=[pltpu.VMEM((B_HEADS, 2 * tq, LANES), BF16), state, state, state],
        compiler_params=_cparams(("parallel", "arbitrary")),
        name="diff_attention",
    )(proj, proj, proj, lq1.reshape(1, -1), lk1.reshape(1, -1), lq2.reshape(1, -1),
      lk2.reshape(1, -1), subln.reshape(1, -1))


ODD_Q = 0
ODD_K = C_QW
ODD_V = ODD_K + C_KV_HEADS * LANES
ODD_QI = ODD_V + C_KV_HEADS * LANES
ODD_KI = ODD_QI + IDX_HEADS * IDX_DIM
ODD_WI = ODD_KI + LANES
ODD_N = ODD_WI + LANES


def _stack_heads(x, lo):
    parts = []
    for pair in range(x.shape[1] // LANES):
        xp = x[:, pair * LANES:(pair + 1) * LANES]
        parts += [jnp.where(lo, xp, 0), jnp.where(lo, 0, xp)]
    return jnp.concatenate(parts, axis=0)


WORD_BITS = 32
PLANE_ROWS = 8 * WORD_BITS
_SWAPS = ((16, 0x0000FFFF), (8, 0x00FF00FF), (4, 0x0F0F0F0F), (2, 0x33333333), (1, 0x55555555))


def _transpose32(words):
    a = list(words)
    for j, m in _SWAPS:
        for k in range(WORD_BITS):
            if k & j == 0:
                t = (lax.shift_right_logical(a[k], jnp.int32(j)) ^ a[k + j]) & jnp.int32(m)
                a[k + j] = a[k + j] ^ t
                a[k] = a[k] ^ lax.shift_left(t, jnp.int32(j))
    return a


def _index_kernel(qi_ref, ki_ref, wi_ref, bias_ref, qs_sc, p_sc, c_sc, g_sc, t_sc,
                  *, tq, tk, nkc, topk, idx_bits):
    i = pl.program_id(1)
    nv = (i * tq) // tk + 1
    ppc = tk // PLANE_ROWS
    hpd = max(1, min(IDX_HEADS, 2048 // tq))
    kf = float(topk)
    krow = lax.broadcasted_iota(jnp.int32, (tk, tq), 0)
    qpos = i * tq + lax.broadcasted_iota(jnp.int32, (tk, tq), 1)
    srow = lax.broadcasted_iota(jnp.int32, (8, tq), 0)
    qs_sc[...] = _stack_heads(qi_ref[0], _lo_lanes())
    w_t = wi_ref[0].astype(F32).T

    def score_body(j, carry):
        start = pl.multiple_of(j * tk, tk)
        k = ki_ref[0, pl.ds(start, tk), :]
        score = jnp.zeros((tk, tq), F32)
        for h0 in range(0, IDX_HEADS, hpd):
            s = _dot_nt(k, qs_sc[h0 * tq:(h0 + hpd) * tq])
            for h in range(hpd):
                score = score + jnp.maximum(s[:, h * tq:(h + 1) * tq], 0.0) * w_t[h0 + h:h0 + h + 1, :]
        score = jnp.where(start + krow <= qpos, score, -jnp.inf)
        bits = lax.bitcast_convert_type(score, jnp.int32)
        key = bits ^ (lax.shift_right_arithmetic(bits, jnp.int32(31)) | jnp.int32(-2 ** 31))
        for t in range(ppc):
            r0 = t * PLANE_ROWS
            planes = _transpose32([key[r0 + v * 8:r0 + v * 8 + 8, :] for v in range(WORD_BITS)])
            for b in range(WORD_BITS):
                p_sc[b, j * ppc + t] = planes[b]
        return carry

    lax.fori_loop(0, nv, score_body, 0)

    def each_tile(fn, init):
        def body(j, carry):
            for t in range(ppc):
                carry = fn(j * ppc + t, carry)
            return carry
        return lax.fori_loop(0, nv, body, init)

    def lane_count(word_fn):
        acc = each_tile(lambda pv, a: a + lax.population_count(word_fn(pv)), jnp.zeros((8, tq), jnp.int32))
        return jnp.sum(acc.astype(F32), axis=0, keepdims=True)

    c_sc[...] = jnp.full(c_sc.shape, -1, jnp.int32)
    g_sc[...] = jnp.zeros(g_sc.shape, jnp.int32)

    def bit_body(t, above):
        b = WORD_BITS - 1 - t
        ones = lane_count(lambda pv: c_sc[pv] & p_sc[b, pv])
        take = above + ones >= kf
        takem = jnp.where(take, -1, 0).astype(jnp.int32)

        def update(pv, carry):
            c = c_sc[pv]
            hi = c & p_sc[b, pv]
            c_sc[pv] = (hi & takem) | ((c ^ hi) & ~takem)
            g_sc[pv] = g_sc[pv] | (hi & ~takem)
            return carry

        each_tile(update, 0)
        return jnp.where(take, above, above + ones)

    above = lax.fori_loop(0, WORD_BITS, bit_body, jnp.zeros((1, tq), F32))

    def below(pv, c):
        nb = jnp.clip(lax.shift_right_arithmetic(c - pv * PLANE_ROWS - srow + 7, jnp.int32(3)), 0, WORD_BITS)
        return jnp.where(nb >= WORD_BITS, -1, lax.shift_left(jnp.int32(1), nb) - 1)

    t_sc[...] = jnp.full(t_sc.shape, -1, jnp.int32)
    need = kf - above
    n_tie = lane_count(lambda pv: c_sc[pv])

    @pl.when(jnp.max(n_tie - need) > 0.0)
    def _():
        def idx_body(t, cur):
            cand = cur | lax.shift_left(jnp.int32(1), idx_bits - 1 - t)
            cnt = lane_count(lambda pv: c_sc[pv] & below(pv, cand))
            return jnp.where(cnt < need, cand, cur)

        cut = lax.fori_loop(0, idx_bits, idx_body, jnp.zeros((1, tq), jnp.int32))

        def keep(pv, carry):
            t_sc[pv] = below(pv, cut + 1)
            return carry

        each_tile(keep, 0)

    for j in range(nkc):
        @pl.when(j < nv)
        def _():
            pieces = []
            for t in range(ppc):
                pv = j * ppc + t
                sel = g_sc[pv] | (c_sc[pv] & t_sc[pv])
                pieces += [lax.shift_right_logical(sel, jnp.int32(v)) & 1 for v in range(WORD_BITS)]
            sel = jnp.concatenate(pieces, axis=0)
            ok = (sel != 0) & (j * tk + krow <= qpos)
            bias_ref[0, j] = jnp.where(ok, 0.0, NEG).T.astype(bias_ref.dtype)

        @pl.when(j >= nv)
        def _():
            bias_ref[0, j] = jnp.full((tq, tk), NEG, bias_ref.dtype)


def _dsa_select(proj, topk, tq=512, tk=512):
    b, s, _ = proj.shape
    nkc = s // tk
    idx_bits = max(1, (s - 1).bit_length())
    qiw = IDX_HEADS * IDX_DIM
    words = pltpu.VMEM((s // PLANE_ROWS, 8, tq), jnp.int32)
    return pl.pallas_call(
        functools.partial(_index_kernel, tq=tq, tk=tk, nkc=nkc, topk=topk, idx_bits=idx_bits),
        out_shape=jax.ShapeDtypeStruct((b, nkc, s, tk), BF16),
        grid=(b, s // tq),
        in_specs=[pl.BlockSpec((1, tq, qiw), lambda bi, i: (bi, i, ODD_QI // qiw)),
                  pl.BlockSpec((1, s, LANES), lambda bi, i: (bi, 0, ODD_KI // LANES)),
                  pl.BlockSpec((1, tq, LANES), lambda bi, i: (bi, i, ODD_WI // LANES))],
        out_specs=pl.BlockSpec((1, nkc, tq, tk), lambda bi, i: (bi, 0, i, 0)),
        scratch_shapes=[pltpu.VMEM((IDX_HEADS * tq, LANES), BF16),
                        pltpu.VMEM((WORD_BITS, s // PLANE_ROWS, 8, tq), jnp.int32),
                        words, words, words],
        compiler_params=_cparams(("parallel", "arbitrary")),
        name="dsa_select",
    )(proj, proj, proj)


def _dsa_kernel(q_ref, k_ref, v_ref, bias_ref, o_ref, qs_sc, m_sc, acc_sc, *, tq, tk):
    i = pl.program_id(1)
    nv = (i * tq) // tk + 1
    lo = _lo_lanes()
    lane = lax.broadcasted_iota(jnp.int32, (1, LANES), 1)
    one_col = jnp.where(lane == HEAD_DIM, 1, 0).astype(BF16)
    gw = C_GROUP * HEAD_DIM
    for g in range(C_KV_HEADS):
        qs_sc[g] = _stack_heads(q_ref[0, :, g * gw:(g + 1) * gw], lo)
    m_sc[...] = jnp.full(m_sc.shape, NEG, F32)
    acc_sc[...] = jnp.zeros(acc_sc.shape, F32)

    def body(j, carry):
        start = pl.multiple_of(j * tk, tk)
        bias = bias_ref[0, j].astype(F32)
        bias = jnp.concatenate([bias] * C_GROUP, axis=0)
        for g in range(C_KV_HEADS):
            k = k_ref[0, pl.ds(start, tk), g * LANES:(g + 1) * LANES]
            v = jnp.where(lo, v_ref[0, pl.ds(start, tk), g * LANES:(g + 1) * LANES], one_col)
            s = _dot_nt(qs_sc[g], k) + bias
            m_prev = m_sc[g]
            m_new = jnp.maximum(m_prev, jnp.max(s, axis=-1, keepdims=True))
            p = jnp.exp2(s - _rep(m_new, tk))
            m_sc[g] = m_new
            acc_sc[g] = (jnp.exp2(m_prev - m_new) * acc_sc[g]
                         + jnp.dot(p.astype(BF16), v, preferred_element_type=F32))
        return carry

    lax.fori_loop(0, nv, body, 0)
    for g in range(C_KV_HEADS):
        acc = acc_sc[g]
        o = acc / acc[:, HEAD_DIM:HEAD_DIM + 1]
        for pair in range(gw // LANES):
            r = 2 * pair * tq
            col = g * gw + pair * LANES
            odd = pltpu.roll(o[r + tq:r + 2 * tq], HEAD_DIM, 1)
            o_ref[0, :, col:col + LANES] = jnp.where(lo, o[r:r + tq], odd).astype(o_ref.dtype)


def _dsa_attention(proj, bias, tq=512, tk=512):
    b, s, _ = proj.shape
    nkc = s // tk
    kvw = C_KV_HEADS * LANES
    state = pltpu.VMEM((C_KV_HEADS, C_GROUP * tq, LANES), F32)
    return pl.pallas_call(
        functools.partial(_dsa_kernel, tq=tq, tk=tk),
        out_shape=jax.ShapeDtypeStruct((b, s, C_QW), BF16),
        grid=(b, s // tq),
        in_specs=[pl.BlockSpec((1, tq, C_QW), lambda bi, i: (bi, i, ODD_Q // C_QW)),
                  pl.BlockSpec((1, s, kvw), lambda bi, i: (bi, 0, ODD_K // kvw)),
                  pl.BlockSpec((1, s, kvw), lambda bi, i: (bi, 0, ODD_V // kvw)),
                  pl.BlockSpec((1, nkc, tq, tk), lambda bi, i: (bi, 0, i, 0))],
        out_specs=pl.BlockSpec((1, tq, C_QW), lambda bi, i: (bi, i, 0)),
        scratch_shapes=[pltpu.VMEM((C_KV_HEADS, C_GROUP * tq, LANES), BF16), state, state],
        compiler_params=_cparams(("parallel", "arbitrary")),
        name="dsa_attention",
    )(proj, proj, proj, bias)


def _mix_ffn_kernel(*refs, n_parts, final_norm):
    x_ref = refs[0]
    parts = refs[1:1 + 2 * n_parts]
    g_ref, wu_ref, wd_ref, gf_ref, o_ref, x1_sc, h_sc, acc_sc = refs[1 + 2 * n_parts:]
    f = pl.program_id(1)

    @pl.when(f == 0)
    def _():
        x1 = x_ref[...]
        for a_ref, w_ref in zip(parts[0::2], parts[1::2]):
            x1 = x1 + jnp.dot(a_ref[...], w_ref[...], preferred_element_type=F32)
        x1_sc[...] = x1
        h_sc[...] = _rms(x1, g_ref[...]).astype(BF16)
        acc_sc[...] = jnp.zeros(acc_sc.shape, F32)

    u = jnp.maximum(jnp.dot(h_sc[...], wu_ref[...], preferred_element_type=F32), 0.0)
    acc_sc[...] += jnp.dot((u * u).astype(BF16), wd_ref[...], preferred_element_type=F32)

    @pl.when(f == pl.num_programs(1) - 1)
    def _():
        y = x1_sc[...] + acc_sc[...]
        o_ref[...] = _rms(y, gf_ref[...]) if final_norm else y


def _mix_ffn(x2, parts, g, wu, wd, g_final, final_norm, tm=1024, tf=512):
    m_rows, d = x2.shape
    tm = min(tm, m_rows)
    dff = wu.shape[1]
    row = lambda cols: pl.BlockSpec((tm, cols), lambda i, f: (i, 0))
    const = lambda shape: pl.BlockSpec(shape, lambda i, f: (0, 0))
    in_specs = [row(d)]
    args = [x2]
    for a, w in parts:
        in_specs += [row(a.shape[1]), const(w.shape)]
        args += [a, w]
    in_specs += [const((1, d)),
                 pl.BlockSpec((d, tf), lambda i, f: (0, f)),
                 pl.BlockSpec((tf, d), lambda i, f: (f, 0)),
                 const((1, d))]
    args += [g.reshape(1, d), wu, wd, g_final.reshape(1, d)]
    return pl.pallas_call(
        functools.partial(_mix_ffn_kernel, n_parts=len(parts), final_norm=final_norm),
        out_shape=jax.ShapeDtypeStruct((m_rows, d), F32),
        grid=(m_rows // tm, dff // tf),
        in_specs=in_specs,
        out_specs=row(d),
        scratch_shapes=[pltpu.VMEM((tm, d), F32), pltpu.VMEM((tm, d), BF16), pltpu.VMEM((tm, d), F32)],
        compiler_params=_cparams(("parallel", "arbitrary")),
        name="mix_ffn",
    )(*args)


Q_SCALE = HEAD_DIM ** -0.5 * math.log2(math.e)


def _even_chunks():
    scale = Q_SCALE
    w = A_W
    return ((0, w, True, scale, 1, 0), (w, w, True, 1.0, 1, w), (2 * w, w, False, 1.0, 1, 2 * w),
            (3 * w, w, True, scale, 0, 0), (4 * w, w, True, 1.0, 0, w), (5 * w, w, False, 1.0, 0, 2 * w))


def _odd_chunks():
    w = 512
    chunks = ((ODD_Q, w, True, Q_SCALE), (ODD_Q + w, w, True, Q_SCALE),
              (ODD_K, w, True, 1.0), (ODD_V, w, False, 1.0),
              (ODD_QI, w, True, IDX_DIM ** -0.5), (ODD_KI, LANES, True, 1.0),
              (ODD_WI, LANES, False, IDX_HEADS ** -0.5))
    return tuple(c + (0, c[0]) for c in chunks)


def _odd_weight(w):
    d = w.shape[0]
    q = w[:, :C_QW]
    k = w[:, C_QW:C_QW + C_KVW].reshape(d, C_KV_HEADS, 1, HEAD_DIM)
    v = w[:, C_QW + C_KVW:C_QW + 2 * C_KVW].reshape(d, C_KV_HEADS, 1, HEAD_DIM)
    dup = lambda t: jnp.broadcast_to(t, (d, C_KV_HEADS, 2, HEAD_DIM)).reshape(d, C_KV_HEADS * LANES)
    o = C_QW + 2 * C_KVW
    qi = w[:, o:o + IDX_HEADS * IDX_DIM]
    ki = w[:, o + IDX_HEADS * IDX_DIM:o + IDX_HEADS * IDX_DIM + IDX_DIM]
    wi = w[:, o + IDX_HEADS * IDX_DIM + IDX_DIM:]
    pad = jnp.zeros((d, LANES - IDX_HEADS), w.dtype)
    return jnp.concatenate([q, dup(k), dup(v), qi, ki, ki, wi, pad], axis=1)


def kernel(x, norm_mix, norm_ffn, w_in_even, w_out_even, lambda_q1, lambda_k1, lambda_q2,
           lambda_k2, diff_subln, w_in_odd, w_out_odd, w_ffn_up, w_ffn_down, norm_final):
    b, s, d = x.shape
    depth = norm_mix.shape[0]
    tables = _rope_tables(s)
    topk = min(TOPK_MAX, s // 4)
    x2 = x.reshape(b * s, d)
    for layer in range(depth):
        if layer % 2 == 0:
            e = layer // 2
            proj_b, proj_a = _norm_proj(x2, norm_mix[layer], w_in_even[e].astype(BF16), tables,
                                        _even_chunks(), s, ((3 * B_W, BF16), (3 * A_W, F32)))
            out_a = _dilated_attention(proj_a.reshape(b, s, 3 * A_W))
            lam_init = 0.8 - 0.6 * math.exp(-0.3 * layer)
            out_b = _diff_attention(proj_b.reshape(b, s, 3 * B_W), lambda_q1[e], lambda_k1[e],
                                    lambda_q2[e], lambda_k2[e], diff_subln[e], lam_init)
            wo = w_out_even[e].astype(BF16)
            parts = [(out_a.reshape(b * s, A_W), wo[:A_W]), (out_b.reshape(b * s, B_W), wo[A_W:])]
        else:
            o = layer // 2
            (proj,) = _norm_proj(x2, norm_mix[layer], _odd_weight(w_in_odd[o]).astype(BF16), tables,
                                 _odd_chunks(), s, ((ODD_N, BF16),))
            proj3 = proj.reshape(b, s, proj.shape[1])
            bias = _dsa_select(proj3, topk)
            out_c = _dsa_attention(proj3, bias)
            parts = [(out_c.reshape(b * s, C_QW), w_out_odd[o].astype(BF16))]
        x2 = _mix_ffn(x2, parts, norm_ffn[layer], w_ffn_up[layer].astype(BF16),
                      w_ffn_down[layer].astype(BF16), norm_final, layer == depth - 1)
    return x2.reshape(b, s, d)
```

```python
import functools
import math

import jax
import jax.numpy as jnp
from jax import lax
from jax.experimental import pallas as pl
from jax.experimental.pallas import tpu as pltpu

D_MODEL = 1024
HEAD_DIM = 64
ROT_DIM = HEAD_DIM // 4
ROPE_THETA = 500000.0
NORM_EPS = 1e-6

A_HEADS = 8
A_PATTERNS = ((128, 1), (512, 4), (2048, 16))
A_W = A_HEADS * HEAD_DIM
B_HEADS = 4
B_VDIM = 2 * HEAD_DIM
B_W = B_HEADS * B_VDIM
C_HEADS = 16
C_KV_HEADS = 4
C_GROUP = C_HEADS // C_KV_HEADS
IDX_HEADS = 8
IDX_DIM = 64
TOPK_MAX = 256
D_FF = 4 * D_MODEL
C_QW = C_HEADS * HEAD_DIM
C_KVW = C_KV_HEADS * HEAD_DIM

LANES = 128
DIL_BLK = 128
NEG = -1e30
VMEM_LIMIT = 56 * 1024 * 1024

BF16 = jnp.bfloat16
F32 = jnp.float32


def _cparams(sem):
    return pltpu.CompilerParams(dimension_semantics=sem, vmem_limit_bytes=VMEM_LIMIT)


def _lo_lanes():
    return lax.broadcasted_iota(jnp.int32, (1, LANES), 1) < HEAD_DIM


def _rep(t, width):
    n = width // LANES
    return t if n == 1 else jnp.concatenate([t] * n, axis=1)


def _dot_nt(a, b):
    return lax.dot_general(a, b, (((1,), (1,)), ((), ())), preferred_element_type=F32)


def _rms(x, g):
    return x * lax.rsqrt(jnp.mean(x * x, axis=-1, keepdims=True) + NORM_EPS) * g


def _softmax_step(s, m_prev, l_prev):
    m_new = jnp.maximum(m_prev, jnp.max(s, axis=-1, keepdims=True))
    alpha = jnp.exp2(m_prev - m_new)
    p = jnp.exp2(s - _rep(m_new, s.shape[1]))
    l_new = alpha * l_prev + jnp.sum(p, axis=-1, keepdims=True)
    return p, m_new, l_new, alpha


def _rope_tables(seq_len):
    pos = jnp.arange(seq_len, dtype=F32)
    inv_freq = jnp.power(ROPE_THETA, -jnp.arange(0, ROT_DIM, 2, dtype=F32) / ROT_DIM)
    ang = pos[:, None] * inv_freq[None, :]
    cos, sin = jnp.cos(ang), jnp.sin(ang)
    half = ROT_DIM // 2
    rest = HEAD_DIM - ROT_DIM
    one = jnp.ones((seq_len, rest), F32)
    z_h = jnp.zeros((seq_len, half), F32)
    z_r = jnp.zeros((seq_len, rest), F32)
    c = jnp.concatenate([cos, cos, one], axis=1)
    sa = jnp.concatenate([z_h, sin, z_r], axis=1)
    sb = jnp.concatenate([-sin, z_h, z_r], axis=1)
    two = lambda t: jnp.concatenate([t, t], axis=1)
    return two(c), two(sa), two(sb)


def _proj_kernel(x_ref, g_ref, w_ref, c_ref, sa_ref, sb_ref, *o_refs, chunks):
    h = _rms(x_ref[...], g_ref[...]).astype(BF16)
    c, sa, sb = c_ref[...], sa_ref[...], sb_ref[...]
    half = ROT_DIM // 2
    for start, width, rope, scale, dest, dstart in chunks:
        acc = jnp.dot(h, w_ref[:, start:start + width], preferred_element_type=F32)
        if rope:
            acc = (acc * _rep(c, width)
                   + pltpu.roll(acc, half, 1) * _rep(sa, width)
                   + pltpu.roll(acc, width - half, 1) * _rep(sb, width))
        if scale != 1.0:
            acc = acc * scale
        o_ref = o_refs[dest]
        o_ref[:, dstart:dstart + width] = acc.astype(o_ref.dtype)


def _norm_proj(x2, g, w, tables, chunks, seq_len, outs, tm=512):
    m_rows, d = x2.shape
    n = w.shape[1]
    nt = seq_len // tm
    tab_spec = pl.BlockSpec((tm, LANES), lambda i: (i % nt, 0))
    return pl.pallas_call(
        functools.partial(_proj_kernel, chunks=chunks),
        out_shape=tuple(jax.ShapeDtypeStruct((m_rows, cols), dt) for cols, dt in outs),
        grid=(m_rows // tm,),
        in_specs=[pl.BlockSpec((tm, d), lambda i: (i, 0)),
                  pl.BlockSpec((1, d), lambda i: (0, 0)),
                  pl.BlockSpec((d, n), lambda i: (0, 0)),
                  tab_spec, tab_spec, tab_spec],
        out_specs=tuple(pl.BlockSpec((tm, cols), lambda i: (i, 0)) for cols, _ in outs),
        compiler_params=_cparams(("parallel",)),
        name="norm_proj",
    )(x2, g.reshape(1, d), w, *tables)


DIL_TILE = DIL_BLK * max(d for _, d in A_PATTERNS)
DIL_BATCH = 4


def _dil_work():
    units = []
    for _, d in reversed(A_PATTERNS):
        span = DIL_BLK * d
        for blk in range(DIL_TILE // span):
            for r in range(d):
                off = blk * span + r
                units.append((d, off, (off - span) % DIL_TILE, blk == 0))
    return [units[i:i + DIL_BATCH] for i in range(0, len(units), DIL_BATCH)]


def _dil_kernel(q_ref, kc_ref, kp_ref, vc_ref, vp_ref, o_ref, m_sc, l_sc, acc_sc):
    has_prev = pl.program_id(1) > 0
    blk = DIL_BLK
    lo = _lo_lanes()
    row = lax.broadcasted_iota(jnp.int32, (2 * blk, 2 * blk), 0)
    col = lax.broadcasted_iota(jnp.int32, (2 * blk, 2 * blk), 1)
    rel = jnp.where(row >= blk, row - blk, row) - col + blk
    band = (rel >= 0) & (rel <= blk)
    band_first = band & ((col >= blk) | has_prev)

    def rows(d, off):
        return pl.ds(off, blk, stride=d) if d > 1 else pl.ds(off, blk)

    for units in _dil_work():
        init = units[0][0] == A_PATTERNS[-1][1]
        s_parts, v_parts = [], []
        for d, off, poff, prev_tile in units:
            q = q_ref[0, rows(d, off), :].astype(BF16)
            qs = jnp.concatenate([jnp.where(lo, q, 0), jnp.where(lo, 0, q)], axis=0)
            kp, vp = (kp_ref, vp_ref) if prev_tile else (kc_ref, vc_ref)
            kcat = jnp.concatenate([kp[0, rows(d, poff), :], kc_ref[0, rows(d, off), :]], axis=0)
            vcat = jnp.concatenate([vp[0, rows(d, poff), :], vc_ref[0, rows(d, off), :]], axis=0)
            s = _dot_nt(qs, kcat.astype(BF16))
            s_parts.append(jnp.where(band_first if prev_tile else band, s, NEG))
            v_parts.append(vcat.astype(BF16))
        s = jnp.concatenate(s_parts, axis=0)
        m_cur = jnp.max(s, axis=-1, keepdims=True)
        n_rows = s.shape[0]
        if init:
            m_new = jnp.broadcast_to(m_cur, (n_rows, LANES))
            p = jnp.exp2(s - m_cur)
            l_new = jnp.broadcast_to(jnp.sum(p, axis=-1, keepdims=True), (n_rows, LANES))
        else:
            state = lambda sc: jnp.concatenate(
                [sc[h, rows(d, off), :] for d, off, _, _ in units for h in range(2)], axis=0)
            m_prev = state(m_sc)
            m_new = jnp.maximum(m_prev, m_cur)
            alpha = jnp.exp2(m_prev - m_new)
            p = jnp.exp2(s - _rep(m_new, 2 * blk))
            l_new = alpha * state(l_sc) + jnp.sum(p, axis=-1, keepdims=True)
            acc_prev = state(acc_sc)
        p = p.astype(BF16)
        for u, (d, off, _, _) in enumerate(units):
            r0 = u * 2 * blk
            pv = jnp.dot(p[r0:r0 + 2 * blk], v_parts[u], preferred_element_type=F32)
            if not init:
                pv = alpha[r0:r0 + 2 * blk] * acc_prev[r0:r0 + 2 * blk] + pv
            for h in range(2):
                sl = slice(r0 + h * blk, r0 + (h + 1) * blk)
                m_sc[h, rows(d, off), :] = m_new[sl]
                l_sc[h, rows(d, off), :] = l_new[sl]
                acc_sc[h, rows(d, off), :] = pv[h * blk:(h + 1) * blk]

    o_ref[0] = jnp.where(lo, acc_sc[0] / l_sc[0], acc_sc[1] / l_sc[1]).astype(o_ref.dtype)


def _dilated_attention(qkv):
    b, s, _ = qkv.shape
    n_pairs = A_W // LANES
    blk = (1, DIL_TILE, LANES)
    cur = lambda c: pl.BlockSpec(blk, lambda bi, n, p: (bi, n, c * n_pairs + p))
    prev = lambda c: pl.BlockSpec(blk, lambda bi, n, p: (bi, jnp.maximum(n - 1, 0), c * n_pairs + p))
    state = pltpu.VMEM((2, DIL_TILE, LANES), F32)
    return pl.pallas_call(
        _dil_kernel,
        out_shape=jax.ShapeDtypeStruct((b, s, A_W), BF16),
        grid=(b, s // DIL_TILE, n_pairs),
        in_specs=[cur(0), cur(1), prev(1), cur(2), prev(2)],
        out_specs=pl.BlockSpec(blk, lambda bi, n, p: (bi, n, p)),
        scratch_shapes=[state, state, state],
        compiler_params=_cparams(("parallel", "parallel", "arbitrary")),
        name="dilated_attention",
    )(qkv, qkv, qkv, qkv, qkv)


def _diff_kernel(q_ref, k_ref, v_ref, lq1, lk1, lq2, lk2, g_ref, o_ref,
                 qs_sc, m_sc, l_sc, acc_sc, *, tq, lam_init):
    i = pl.program_id(1)
    lo = _lo_lanes()
    for h in range(B_HEADS):
        q = q_ref[0, :, h * LANES:(h + 1) * LANES]
        qs_sc[h] = jnp.concatenate([jnp.where(lo, q, 0), jnp.where(lo, 0, q)], axis=0)
    m_sc[...] = jnp.full(m_sc.shape, NEG, F32)
    l_sc[...] = jnp.zeros(l_sc.shape, F32)
    acc_sc[...] = jnp.zeros(acc_sc.shape, F32)

    def step(j, masked):
        start = pl.multiple_of(j * tq, tq)
        for h in range(B_HEADS):
            k = k_ref[0, pl.ds(start, tq), h * LANES:(h + 1) * LANES]
            v = v_ref[0, pl.ds(start, tq), h * LANES:(h + 1) * LANES]
            s = _dot_nt(qs_sc[h], k)
            if masked:
                row = lax.broadcasted_iota(jnp.int32, (2 * tq, tq), 0)
                col = lax.broadcasted_iota(jnp.int32, (2 * tq, tq), 1)
                s = jnp.where(col <= jnp.where(row >= tq, row - tq, row), s, NEG)
            p, m_new, l_new, alpha = _softmax_step(s, m_sc[h], l_sc[h])
            m_sc[h] = m_new
            l_sc[h] = l_new
            acc_sc[h] = alpha * acc_sc[h] + jnp.dot(p.astype(BF16), v, preferred_element_type=F32)

    def body(j, carry):
        step(j, False)
        return carry

    lax.fori_loop(0, i, body, 0)
    step(i, True)

    lam = (jnp.exp(jnp.sum(lq1[...] * lk1[...], axis=-1, keepdims=True))
           - jnp.exp(jnp.sum(lq2[...] * lk2[...], axis=-1, keepdims=True)) + lam_init)
    for h in range(B_HEADS):
        o = acc_sc[h] / l_sc[h]
        o = o[:tq] - lam * o[tq:]
        o_ref[0, :, h * LANES:(h + 1) * LANES] = (_rms(o, g_ref[...]) * (1.0 - lam_init)).astype(o_ref.dtype)


def _diff_attention(proj, lq1, lk1, lq2, lk2, subln, lam_init, tq=512):
    b, s, _ = proj.shape
    vec = pl.BlockSpec((1, HEAD_DIM), lambda bi, i: (0, 0))
    state = pltpu.VMEM((B_HEADS, 2 * tq, LANES), F32)
    return pl.pallas_call(
        functools.partial(_diff_kernel, tq=tq, lam_init=lam_init),
        out_shape=jax.ShapeDtypeStruct((b, s, B_W), BF16),
        grid=(b, s // tq),
        in_specs=[pl.BlockSpec((1, tq, B_W), lambda bi, i: (bi, i, 0)),
                  pl.BlockSpec((1, s, B_W), lambda bi, i: (bi, 0, 1)),
                  pl.BlockSpec((1, s, B_W), lambda bi, i: (bi, 0, 2)),
                  vec, vec, vec, vec,
                  pl.BlockSpec((1, B_VDIM), lambda bi, i: (0, 0))],
        out_specs=pl.BlockSpec((1, tq, B_W), lambda bi, i: (bi, i, 0)),
        scratch_shapes=[pltpu.VMEM((B_HEADS, 2 * tq, LANES), BF16), state, state, state],
        compiler_params=_cparams(("parallel", "arbitrary")),
        name="diff_attention",
    )(proj, proj, proj, lq1.reshape(1, -1), lk1.reshape(1, -1), lq2.reshape(1, -1),
      lk2.reshape(1, -1), subln.reshape(1, -1))


ODD_Q = 0
ODD_K = C_QW
ODD_V = ODD_K + C_KV_HEADS * LANES
ODD_QI = ODD_V + C_KV_HEADS * LANES
ODD_KI = ODD_QI + IDX_HEADS * IDX_DIM
ODD_WI = ODD_KI + LANES
ODD_N = ODD_WI + LANES


def _stack_heads(x, lo):
    parts = []
    for pair in range(x.shape[1] // LANES):
        xp = x[:, pair * LANES:(pair + 1) * LANES]
        parts += [jnp.where(lo, xp, 0), jnp.where(lo, 0, xp)]
    return jnp.concatenate(parts, axis=0)


WORD_BITS = 32
PLANE_ROWS = 8 * WORD_BITS
_SWAPS = ((16, 0x0000FFFF), (8, 0x00FF00FF), (4, 0x0F0F0F0F), (2, 0x33333333), (1, 0x55555555))


def _transpose32(words):
    a = list(words)
    for j, m in _SWAPS:
        for k in range(WORD_BITS):
            if k & j == 0:
                t = (lax.shift_right_logical(a[k], jnp.int32(j)) ^ a[k + j]) & jnp.int32(m)
                a[k + j] = a[k + j] ^ t
                a[k] = a[k] ^ lax.shift_left(t, jnp.int32(j))
    return a


def _index_kernel(qi_ref, ki_ref, wi_ref, bias_ref, qs_sc, p_sc, c_sc, g_sc, t_sc,
                  *, tq, tk, nkc, topk, idx_bits):
    i = pl.program_id(1)
    nv = (i * tq) // tk + 1
    ppc = tk // PLANE_ROWS
    hpd = max(1, min(IDX_HEADS, 2048 // tq))
    kf = float(topk)
    krow = lax.broadcasted_iota(jnp.int32, (tk, tq), 0)
    qpos = i * tq + lax.broadcasted_iota(jnp.int32, (tk, tq), 1)
    srow = lax.broadcasted_iota(jnp.int32, (8, tq), 0)
    qs_sc[...] = _stack_heads(qi_ref[0], _lo_lanes())
    w_t = wi_ref[0].astype(F32).T

    def score_body(j, carry):
        start = pl.multiple_of(j * tk, tk)
        k = ki_ref[0, pl.ds(start, tk), :]
        score = jnp.zeros((tk, tq), F32)
        for h0 in range(0, IDX_HEADS, hpd):
            s = _dot_nt(k, qs_sc[h0 * tq:(h0 + hpd) * tq])
            for h in range(hpd):
                score = score + jnp.maximum(s[:, h * tq:(h + 1) * tq], 0.0) * w_t[h0 + h:h0 + h + 1, :]
        score = jnp.where(start + krow <= qpos, score, -jnp.inf)
        bits = lax.bitcast_convert_type(score, jnp.int32)
        key = bits ^ (lax.shift_right_arithmetic(bits, jnp.int32(31)) | jnp.int32(-2 ** 31))
        for t in range(ppc):
            r0 = t * PLANE_ROWS
            planes = _transpose32([key[r0 + v * 8:r0 + v * 8 + 8, :] for v in range(WORD_BITS)])
            for b in range(WORD_BITS):
                p_sc[b, j * ppc + t] = planes[b]
        return carry

    lax.fori_loop(0, nv, score_body, 0)

    def each_tile(fn, init):
        def body(j, carry):
            for t in range(ppc):
                carry = fn(j * ppc + t, carry)
            return carry
        return lax.fori_loop(0, nv, body, init)

    def lane_count(word_fn):
        acc = each_tile(lambda pv, a: a + lax.population_count(word_fn(pv)), jnp.zeros((8, tq), jnp.int32))
        return jnp.sum(acc.astype(F32), axis=0, keepdims=True)

    c_sc[...] = jnp.full(c_sc.shape, -1, jnp.int32)
    g_sc[...] = jnp.zeros(g_sc.shape, jnp.int32)

    def bit_body(t, above):
        b = WORD_BITS - 1 - t
        ones = lane_count(lambda pv: c_sc[pv] & p_sc[b, pv])
        take = above + ones >= kf
        takem = jnp.where(take, -1, 0).astype(jnp.int32)

        def update(pv, carry):
            c = c_sc[pv]
            hi = c & p_sc[b, pv]
            c_sc[pv] = (hi & takem) | ((c ^ hi) & ~takem)
            g_sc[pv] = g_sc[pv] | (hi & ~takem)
            return carry

        each_tile(update, 0)
        return jnp.where(take, above, above + ones)

    above = lax.fori_loop(0, WORD_BITS, bit_body, jnp.zeros((1, tq), F32))

    def below(pv, c):
        nb = jnp.clip(lax.shift_right_arithmetic(c - pv * PLANE_ROWS - srow + 7, jnp.int32(3)), 0, WORD_BITS)
        return jnp.where(nb >= WORD_BITS, -1, lax.shift_left(jnp.int32(1), nb) - 1)

    t_sc[...] = jnp.full(t_sc.shape, -1, jnp.int32)
    need = kf - above
    n_tie = lane_count(lambda pv: c_sc[pv])

    @pl.when(jnp.max(n_tie - need) > 0.0)
    def _():
        def idx_body(t, cur):
            cand = cur | lax.shift_left(jnp.int32(1), idx_bits - 1 - t)
            cnt = lane_count(lambda pv: c_sc[pv] & below(pv, cand))
            return jnp.where(cnt < need, cand, cur)

        cut = lax.fori_loop(0, idx_bits, idx_body, jnp.zeros((1, tq), jnp.int32))

        def keep(pv, carry):
            t_sc[pv] = below(pv, cut + 1)
            return carry

        each_tile(keep, 0)

    for j in range(nkc):
        @pl.when(j < nv)
        def _():
            pieces = []
            for t in range(ppc):
                pv = j * ppc + t
                sel = g_sc[pv] | (c_sc[pv] & t_sc[pv])
                pieces += [lax.shift_right_logical(sel, jnp.int32(v)) & 1 for v in range(WORD_BITS)]
            sel = jnp.concatenate(pieces, axis=0)
            ok = (sel != 0) & (j * tk + krow <= qpos)
            bias_ref[0, j] = jnp.where(ok, 0.0, NEG).T.astype(bias_ref.dtype)

        @pl.when(j >= nv)
        def _():
            bias_ref[0, j] = jnp.full((tq, tk), NEG, bias_ref.dtype)


def _dsa_select(proj, topk, tq=512, tk=512):
    b, s, _ = proj.shape
    nkc = s // tk
    idx_bits = max(1, (s - 1).bit_length())
    qiw = IDX_HEADS * IDX_DIM
    words = pltpu.VMEM((s // PLANE_ROWS, 8, tq), jnp.int32)
    return pl.pallas_call(
        functools.partial(_index_kernel, tq=tq, tk=tk, nkc=nkc, topk=topk, idx_bits=idx_bits),
        out_shape=jax.ShapeDtypeStruct((b, nkc, s, tk), BF16),
        grid=(b, s // tq),
        in_specs=[pl.BlockSpec((1, tq, qiw), lambda bi, i: (bi, i, ODD_QI // qiw)),
                  pl.BlockSpec((1, s, LANES), lambda bi, i: (bi, 0, ODD_KI // LANES)),
                  pl.BlockSpec((1, tq, LANES), lambda bi, i: (bi, i, ODD_WI // LANES))],
        out_specs=pl.BlockSpec((1, nkc, tq, tk), lambda bi, i: (bi, 0, i, 0)),
        scratch_shapes=[pltpu.VMEM((IDX_HEADS * tq, LANES), BF16),
                        pltpu.VMEM((WORD_BITS, s // PLANE_ROWS, 8, tq), jnp.int32),
                        words, words, words],
        compiler_params=_cparams(("parallel", "arbitrary")),
        name="dsa_select",
    )(proj, proj, proj)


def _dsa_kernel(q_ref, k_ref, v_ref, bias_ref, o_ref, qs_sc, m_sc, acc_sc, *, tq, tk):
    i = pl.program_id(1)
    nv = (i * tq) // tk + 1
    lo = _lo_lanes()
    gw = C_GROUP * HEAD_DIM
    for g in range(C_KV_HEADS):
        qs_sc[g] = _stack_heads(q_ref[0, :, g * gw:(g + 1) * gw], lo)
    m_sc[...] = jnp.full(m_sc.shape, NEG, F32)
    acc_sc[...] = jnp.zeros(acc_sc.shape, F32)

    def body(j, carry):
        start = pl.multiple_of(j * tk, tk)
        bias = bias_ref[0, j].astype(F32)
        bias = jnp.concatenate([bias] * C_GROUP, axis=0)
        for g in range(C_KV_HEADS):
            k = k_ref[0, pl.ds(start, tk), g * LANES:(g + 1) * LANES]
            v = jnp.where(lo, v_ref[0, pl.ds(start, tk), g * LANES:(g + 1) * LANES], 1)
            s = _dot_nt(qs_sc[g], k) + bias
            m_prev = m_sc[g]
            m_new = jnp.maximum(m_prev, jnp.max(s, axis=-1, keepdims=True))
            p = jnp.exp2(s - _rep(m_new, tk))
            m_sc[g] = m_new
            acc_sc[g] = (jnp.exp2(m_prev - m_new) * acc_sc[g]
                         + jnp.dot(p.astype(BF16), v, preferred_element_type=F32))
        return carry

    lax.fori_loop(0, nv, body, 0)
    for g in range(C_KV_HEADS):
        acc = acc_sc[g]
        swapped = pltpu.roll(acc, HEAD_DIM, 1)
        for pair in range(gw // LANES):
            r = 2 * pair * tq
            col = g * gw + pair * LANES
            even = acc[r:r + tq] / swapped[r:r + tq]
            odd = swapped[r + tq:r + 2 * tq] / acc[r + tq:r + 2 * tq]
            o_ref[0, :, col:col + LANES] = jnp.where(lo, even, odd).astype(o_ref.dtype)


def _dsa_attention(proj, bias, tq=512, tk=512):
    b, s, _ = proj.shape
    nkc = s // tk
    kvw = C_KV_HEADS * LANES
    state = pltpu.VMEM((C_KV_HEADS, C_GROUP * tq, LANES), F32)
    return pl.pallas_call(
        functools.partial(_dsa_kernel, tq=tq, tk=tk),
        out_shape=jax.ShapeDtypeStruct((b, s, C_QW), BF16),
        grid=(b, s // tq),
        in_specs=[pl.BlockSpec((1, tq, C_QW), lambda bi, i: (bi, i, ODD_Q // C_QW)),
                  pl.BlockSpec((1, s, kvw), lambda bi, i: (bi, 0, ODD_K // kvw)),
                  pl.BlockSpec((1, s, kvw), lambda bi, i: (bi, 0, ODD_V // kvw)),
                  pl.BlockSpec((1, nkc, tq, tk), lambda bi, i: (bi, 0, i, 0))],
        out_specs=pl.BlockSpec((1, tq, C_QW), lambda bi, i: (bi, i, 0)),
        scratch_shapes=[pltpu.VMEM((C_KV_HEADS, C_GROUP * tq, LANES), BF16), state, state],
        compiler_params=_cparams(("parallel", "arbitrary")),
        name="dsa_attention",
    )(proj, proj, proj, bias)


def _mix_ffn_kernel(*refs, n_parts, final_norm):
    x_ref = refs[0]
    parts = refs[1:1 + 2 * n_parts]
    g_ref, wu_ref, wd_ref, gf_ref, o_ref, x1_sc, h_sc, acc_sc = refs[1 + 2 * n_parts:]
    f = pl.program_id(1)

    @pl.when(f == 0)
    def _():
        x1 = x_ref[...]
        for a_ref, w_ref in zip(parts[0::2], parts[1::2]):
            x1 = x1 + jnp.dot(a_ref[...], w_ref[...], preferred_element_type=F32)
        x1_sc[...] = x1
        h_sc[...] = _rms(x1, g_ref[...]).astype(BF16)
        acc_sc[...] = jnp.zeros(acc_sc.shape, F32)

    u = jnp.maximum(jnp.dot(h_sc[...], wu_ref[...], preferred_element_type=F32), 0.0)
    acc_sc[...] += jnp.dot((u * u).astype(BF16), wd_ref[...], preferred_element_type=F32)

    @pl.when(f == pl.num_programs(1) - 1)
    def _():
        y = x1_sc[...] + acc_sc[...]
        o_ref[...] = _rms(y, gf_ref[...]) if final_norm else y


def _mix_ffn(x2, parts, g, wu, wd, g_final, final_norm, tm=1024, tf=512):
    m_rows, d = x2.shape
    tm = min(tm, m_rows)
    dff = wu.shape[1]
    row = lambda cols: pl.BlockSpec((tm, cols), lambda i, f: (i, 0))
    const = lambda shape: pl.BlockSpec(shape, lambda i, f: (0, 0))
    in_specs = [row(d)]
    args = [x2]
    for a, w in parts:
        in_specs += [row(a.shape[1]), const(w.shape)]
        args += [a, w]
    in_specs += [const((1, d)),
                 pl.BlockSpec((d, tf), lambda i, f: (0, f)),
                 pl.BlockSpec((tf, d), lambda i, f: (f, 0)),
                 const((1, d))]
    args += [g.reshape(1, d), wu, wd, g_final.reshape(1, d)]
    return pl.pallas_call(
        functools.partial(_mix_ffn_kernel, n_parts=len(parts), final_norm=final_norm),
        out_shape=jax.ShapeDtypeStruct((m_rows, d), F32),
        grid=(m_rows // tm, dff // tf),
        in_specs=in_specs,
        out_specs=row(d),
        scratch_shapes=[pltpu.VMEM((tm, d), F32), pltpu.VMEM((tm, d), BF16), pltpu.VMEM((tm, d), F32)],
        compiler_params=_cparams(("parallel", "arbitrary")),
        name="mix_ffn",
    )(*args)


Q_SCALE = HEAD_DIM ** -0.5 * math.log2(math.e)


def _even_chunks():
    scale = Q_SCALE
    w = A_W
    return ((0, w, True, scale, 1, 0), (w, w, True, 1.0, 1, w), (2 * w, w, False, 1.0, 1, 2 * w),
            (3 * w, w, True, scale, 0, 0), (4 * w, w, True, 1.0, 0, w), (5 * w, w, False, 1.0, 0, 2 * w))


def _odd_chunks():
    w = 512
    chunks = ((ODD_Q, w, True, Q_SCALE), (ODD_Q + w, w, True, Q_SCALE),
              (ODD_K, w, True, 1.0), (ODD_V, w, False, 1.0),
              (ODD_QI, w, True, IDX_DIM ** -0.5), (ODD_KI, LANES, True, 1.0),
              (ODD_WI, LANES, False, IDX_HEADS ** -0.5))
    return tuple(c + (0, c[0]) for c in chunks)


def _odd_weight(w):
    d = w.shape[0]
    q = w[:, :C_QW]
    k = w[:, C_QW:C_QW + C_KVW].reshape(d, C_KV_HEADS, 1, HEAD_DIM)
    v = w[:, C_QW + C_KVW:C_QW + 2 * C_KVW].reshape(d, C_KV_HEADS, 1, HEAD_DIM)
    dup = lambda t: jnp.broadcast_to(t, (d, C_KV_HEADS, 2, HEAD_DIM)).reshape(d, C_KV_HEADS * LANES)
    o = C_QW + 2 * C_KVW
    qi = w[:, o:o + IDX_HEADS * IDX_DIM]
    ki = w[:, o + IDX_HEADS * IDX_DIM:o + IDX_HEADS * IDX_DIM + IDX_DIM]
    wi = w[:, o + IDX_HEADS * IDX_DIM + IDX_DIM:]
    pad = jnp.zeros((d, LANES - IDX_HEADS), w.dtype)
    return jnp.concatenate([q, dup(k), dup(v), qi, ki, ki, wi, pad], axis=1)


def kernel(x, norm_mix, norm_ffn, w_in_even, w_out_even, lambda_q1, lambda_k1, lambda_q2,
           lambda_k2, diff_subln, w_in_odd, w_out_odd, w_ffn_up, w_ffn_down, norm_final):
    b, s, d = x.shape
    depth = norm_mix.shape[0]
    tables = _rope_tables(s)
    topk = min(TOPK_MAX, s // 4)
    x2 = x.reshape(b * s, d)
    for layer in range(depth):
        if layer % 2 == 0:
            e = layer // 2
            proj_b, proj_a = _norm_proj(x2, norm_mix[layer], w_in_even[e].astype(BF16), tables,
                                        _even_chunks(), s, ((3 * B_W, BF16), (3 * A_W, F32)))
            out_a = _dilated_attention(proj_a.reshape(b, s, 3 * A_W))
            lam_init = 0.8 - 0.6 * math.exp(-0.3 * layer)
            out_b = _diff_attention(proj_b.reshape(b, s, 3 * B_W), lambda_q1[e], lambda_k1[e],
                                    lambda_q2[e], lambda_k2[e], diff_subln[e], lam_init)
            wo = w_out_even[e].astype(BF16)
            parts = [(out_a.reshape(b * s, A_W), wo[:A_W]), (out_b.reshape(b * s, B_W), wo[A_W:])]
        else:
            o = layer // 2
            (proj,) = _norm_proj(x2, norm_mix[layer], _odd_weight(w_in_odd[o]).astype(BF16), tables,
                                 _odd_chunks(), s, ((ODD_N, BF16),))
            proj3 = proj.reshape(b, s, proj.shape[1])
            bias = _dsa_select(proj3, topk)
            out_c = _dsa_attention(proj3, bias)
            parts = [(out_c.reshape(b * s, C_QW), w_out_odd[o].astype(BF16))]
        x2 = _mix_ffn(x2, parts, norm_ffn[layer], w_ffn_up[layer].astype(BF16),
                      w_ffn_down[layer].astype(BF16), norm_final, layer == depth - 1)
    return x2.reshape(b, s, d)
```

```python
import functools
import math

import jax
import jax.numpy as jnp
from jax import lax
from jax.experimental import pallas as pl
from jax.experimental.pallas import tpu as pltpu

D_MODEL = 1024
HEAD_DIM = 64
ROT_DIM = HEAD_DIM // 4
ROPE_THETA = 500000.0
NORM_EPS = 1e-6

A_HEADS = 8
A_PATTERNS = ((128, 1), (512, 4), (2048, 16))
A_W = A_HEADS * HEAD_DIM
B_HEADS = 4
B_VDIM = 2 * HEAD_DIM
B_W = B_HEADS * B_VDIM
C_HEADS = 16
C_KV_HEADS = 4
C_GROUP = C_HEADS // C_KV_HEADS
IDX_HEADS = 8
IDX_DIM = 64
TOPK_MAX = 256
D_FF = 4 * D_MODEL
C_QW = C_HEADS * HEAD_DIM
C_KVW = C_KV_HEADS * HEAD_DIM

LANES = 128
DIL_BLK = 128
NEG = -1e30
V7X_VMEM_BYTES = 64 * 1024 * 1024
VMEM_LIMIT = V7X_VMEM_BYTES * 7 // 8

PROJ_ROWS = 512
ATTN_ROWS = 512
ATTN_KEYS = 512
FFN_ROWS = 1024
FFN_HIDDEN = 512
SCORE_COLS = 2048

BF16 = jnp.bfloat16
F32 = jnp.float32


def _cparams(sem):
    return pltpu.CompilerParams(dimension_semantics=sem, vmem_limit_bytes=VMEM_LIMIT)


def _lo_lanes():
    return lax.broadcasted_iota(jnp.int32, (1, LANES), 1) < HEAD_DIM


def _rep(t, width):
    n = width // LANES
    return t if n == 1 else jnp.concatenate([t] * n, axis=1)


def _dot_nt(a, b):
    return lax.dot_general(a, b, (((1,), (1,)), ((), ())), preferred_element_type=F32)


def _rms(x, g):
    return x * lax.rsqrt(jnp.mean(x * x, axis=-1, keepdims=True) + NORM_EPS) * g


def _softmax_step(s, m_prev, l_prev):
    m_new = jnp.maximum(m_prev, jnp.max(s, axis=-1, keepdims=True))
    alpha = jnp.exp2(m_prev - m_new)
    p = jnp.exp2(s - _rep(m_new, s.shape[1]))
    l_new = alpha * l_prev + jnp.sum(p, axis=-1, keepdims=True)
    return p, m_new, l_new, alpha


def _rope_tables(seq_len):
    pos = jnp.arange(seq_len, dtype=F32)
    inv_freq = jnp.power(ROPE_THETA, -jnp.arange(0, ROT_DIM, 2, dtype=F32) / ROT_DIM)
    ang = pos[:, None] * inv_freq[None, :]
    cos, sin = jnp.cos(ang), jnp.sin(ang)
    half = ROT_DIM // 2
    rest = HEAD_DIM - ROT_DIM
    one = jnp.ones((seq_len, rest), F32)
    z_h = jnp.zeros((seq_len, half), F32)
    z_r = jnp.zeros((seq_len, rest), F32)
    c = jnp.concatenate([cos, cos, one], axis=1)
    sa = jnp.concatenate([z_h, sin, z_r], axis=1)
    sb = jnp.concatenate([-sin, z_h, z_r], axis=1)
    two = lambda t: jnp.concatenate([t, t], axis=1)
    return two(c), two(sa), two(sb)


def _proj_kernel(x_ref, g_ref, w_ref, c_ref, sa_ref, sb_ref, *o_refs, chunks):
    h = _rms(x_ref[...], g_ref[...]).astype(BF16)
    c, sa, sb = c_ref[...], sa_ref[...], sb_ref[...]
    half = ROT_DIM // 2
    for start, width, rope, scale, dest, dstart in chunks:
        acc = jnp.dot(h, w_ref[:, start:start + width], preferred_element_type=F32)
        if rope:
            acc = (acc * _rep(c, width)
                   + pltpu.roll(acc, half, 1) * _rep(sa, width)
                   + pltpu.roll(acc, width - half, 1) * _rep(sb, width))
        if scale != 1.0:
            acc = acc * scale
        o_ref = o_refs[dest]
        o_ref[:, dstart:dstart + width] = acc.astype(o_ref.dtype)


def _norm_proj(x2, g, w, tables, chunks, seq_len, outs, tm=PROJ_ROWS):
    m_rows, d = x2.shape
    n = w.shape[1]
    nt = seq_len // tm
    tab_spec = pl.BlockSpec((tm, LANES), lambda i: (i % nt, 0))
    return pl.pallas_call(
        functools.partial(_proj_kernel, chunks=chunks),
        out_shape=tuple(jax.ShapeDtypeStruct((m_rows, cols), dt) for cols, dt in outs),
        grid=(m_rows // tm,),
        in_specs=[pl.BlockSpec((tm, d), lambda i: (i, 0)),
                  pl.BlockSpec((1, d), lambda i: (0, 0)),
                  pl.BlockSpec((d, n), lambda i: (0, 0)),
                  tab_spec, tab_spec, tab_spec],
        out_specs=tuple(pl.BlockSpec((tm, cols), lambda i: (i, 0)) for cols, _ in outs),
        compiler_params=_cparams(("parallel",)),
        name="norm_proj",
    )(x2, g.reshape(1, d), w, *tables)


DIL_TILE = DIL_BLK * max(d for _, d in A_PATTERNS)
DIL_BATCH = 4


def _dil_work():
    units = []
    for _, d in reversed(A_PATTERNS):
        span = DIL_BLK * d
        for blk in range(DIL_TILE // span):
            for r in range(d):
                off = blk * span + r
                units.append((d, off, (off - span) % DIL_TILE, blk == 0))
    return [units[i:i + DIL_BATCH] for i in range(0, len(units), DIL_BATCH)]


def _dil_kernel(q_ref, kc_ref, kp_ref, vc_ref, vp_ref, o_ref, m_sc, l_sc, acc_sc):
    has_prev = pl.program_id(1) > 0
    blk = DIL_BLK
    lo = _lo_lanes()
    row = lax.broadcasted_iota(jnp.int32, (2 * blk, 2 * blk), 0)
    col = lax.broadcasted_iota(jnp.int32, (2 * blk, 2 * blk), 1)
    rel = jnp.where(row >= blk, row - blk, row) - col + blk
    band = (rel >= 0) & (rel <= blk)
    band_first = band & ((col >= blk) | has_prev)

    def rows(d, off):
        return pl.ds(off, blk, stride=d) if d > 1 else pl.ds(off, blk)

    for units in _dil_work():
        init = units[0][0] == A_PATTERNS[-1][1]
        s_parts, v_parts = [], []
        for d, off, poff, prev_tile in units:
            q = q_ref[0, rows(d, off), :].astype(BF16)
            qs = jnp.concatenate([jnp.where(lo, q, 0), jnp.where(lo, 0, q)], axis=0)
            kp, vp = (kp_ref, vp_ref) if prev_tile else (kc_ref, vc_ref)
            kcat = jnp.concatenate([kp[0, rows(d, poff), :], kc_ref[0, rows(d, off), :]], axis=0)
            vcat = jnp.concatenate([vp[0, rows(d, poff), :], vc_ref[0, rows(d, off), :]], axis=0)
            s = _dot_nt(qs, kcat.astype(BF16))
            s_parts.append(jnp.where(band_first if prev_tile else band, s, NEG))
            v_parts.append(vcat.astype(BF16))
        s = jnp.concatenate(s_parts, axis=0)
        m_cur = jnp.max(s, axis=-1, keepdims=True)
        n_rows = s.shape[0]
        if init:
            m_new = jnp.broadcast_to(m_cur, (n_rows, LANES))
            p = jnp.exp2(s - m_cur)
            l_new = jnp.broadcast_to(jnp.sum(p, axis=-1, keepdims=True), (n_rows, LANES))
        else:
            state = lambda sc: jnp.concatenate(
                [sc[h, rows(d, off), :] for d, off, _, _ in units for h in range(2)], axis=0)
            m_prev = state(m_sc)
            m_new = jnp.maximum(m_prev, m_cur)
            alpha = jnp.exp2(m_prev - m_new)
            p = jnp.exp2(s - _rep(m_new, 2 * blk))
            l_new = alpha * state(l_sc) + jnp.sum(p, axis=-1, keepdims=True)
            acc_prev = state(acc_sc)
        p = p.astype(BF16)
        for u, (d, off, _, _) in enumerate(units):
            r0 = u * 2 * blk
            pv = jnp.dot(p[r0:r0 + 2 * blk], v_parts[u], preferred_element_type=F32)
            if not init:
                pv = alpha[r0:r0 + 2 * blk] * acc_prev[r0:r0 + 2 * blk] + pv
            for h in range(2):
                sl = slice(r0 + h * blk, r0 + (h + 1) * blk)
                m_sc[h, rows(d, off), :] = m_new[sl]
                l_sc[h, rows(d, off), :] = l_new[sl]
                acc_sc[h, rows(d, off), :] = pv[h * blk:(h + 1) * blk]

    o_ref[0] = jnp.where(lo, acc_sc[0] / l_sc[0], acc_sc[1] / l_sc[1]).astype(o_ref.dtype)


def _dilated_attention(qkv):
    b, s, _ = qkv.shape
    n_pairs = A_W // LANES
    blk = (1, DIL_TILE, LANES)
    cur = lambda c: pl.BlockSpec(blk, lambda bi, n, p: (bi, n, c * n_pairs + p))
    prev = lambda c: pl.BlockSpec(blk, lambda bi, n, p: (bi, jnp.maximum(n - 1, 0), c * n_pairs + p))
    state = pltpu.VMEM((2, DIL_TILE, LANES), F32)
    return pl.pallas_call(
        _dil_kernel,
        out_shape=jax.ShapeDtypeStruct((b, s, A_W), BF16),
        grid=(b, s // DIL_TILE, n_pairs),
        in_specs=[cur(0), cur(1), prev(1), cur(2), prev(2)],
        out_specs=pl.BlockSpec(blk, lambda bi, n, p: (bi, n, p)),
        scratch_shapes=[state, state, state],
        compiler_params=_cparams(("parallel", "parallel", "arbitrary")),
        name="dilated_attention",
    )(qkv, qkv, qkv, qkv, qkv)


def _diff_kernel(q_ref, k_ref, v_ref, lq1, lk1, lq2, lk2, g_ref, o_ref,
                 qs_sc, m_sc, l_sc, acc_sc, *, tq, lam_init):
    i = pl.program_id(1)
    lo = _lo_lanes()
    for h in range(B_HEADS):
        q = q_ref[0, :, h * LANES:(h + 1) * LANES]
        qs_sc[h] = jnp.concatenate([jnp.where(lo, q, 0), jnp.where(lo, 0, q)], axis=0)
    m_sc[...] = jnp.full(m_sc.shape, NEG, F32)
    l_sc[...] = jnp.zeros(l_sc.shape, F32)
    acc_sc[...] = jnp.zeros(acc_sc.shape, F32)

    def step(j, masked):
        start = pl.multiple_of(j * tq, tq)
        for h in range(B_HEADS):
            k = k_ref[0, pl.ds(start, tq), h * LANES:(h + 1) * LANES]
            v = v_ref[0, pl.ds(start, tq), h * LANES:(h + 1) * LANES]
            s = _dot_nt(qs_sc[h], k)
            if masked:
                row = lax.broadcasted_iota(jnp.int32, (2 * tq, tq), 0)
                col = lax.broadcasted_iota(jnp.int32, (2 * tq, tq), 1)
                s = jnp.where(col <= jnp.where(row >= tq, row - tq, row), s, NEG)
            p, m_new, l_new, alpha = _softmax_step(s, m_sc[h], l_sc[h])
            m_sc[h] = m_new
            l_sc[h] = l_new
            acc_sc[h] = alpha * acc_sc[h] + jnp.dot(p.astype(BF16), v, preferred_element_type=F32)

    def body(j, carry):
        step(j, False)
        return carry

    lax.fori_loop(0, i, body, 0)
    step(i, True)

    lam = (jnp.exp(jnp.sum(lq1[...] * lk1[...], axis=-1, keepdims=True))
           - jnp.exp(jnp.sum(lq2[...] * lk2[...], axis=-1, keepdims=True)) + lam_init)
    for h in range(B_HEADS):
        o = acc_sc[h] / l_sc[h]
        o = o[:tq] - lam * o[tq:]
        o_ref[0, :, h * LANES:(h + 1) * LANES] = (_rms(o, g_ref[...]) * (1.0 - lam_init)).astype(o_ref.dtype)


def _diff_attention(proj, lq1, lk1, lq2, lk2, subln, lam_init, tq=ATTN_ROWS):
    b, s, _ = proj.shape
    vec = pl.BlockSpec((1, HEAD_DIM), lambda bi, i: (0, 0))
    state = pltpu.VMEM((B_HEADS, 2 * tq, LANES), F32)
    return pl.pallas_call(
        functools.partial(_diff_kernel, tq=tq, lam_init=lam_init),
        out_shape=jax.ShapeDtypeStruct((b, s, B_W), BF16),
        grid=(b, s // tq),
        in_specs=[pl.BlockSpec((1, tq, B_W), lambda bi, i: (bi, i, 0)),
                  pl.BlockSpec((1, s, B_W), lambda bi, i: (bi, 0, 1)),
                  pl.BlockSpec((1, s, B_W), lambda bi, i: (bi, 0, 2)),
                  vec, vec, vec, vec,
                  pl.BlockSpec((1, B_VDIM), lambda bi, i: (0, 0))],
        out_specs=pl.BlockSpec((1, tq, B_W), lambda bi, i: (bi, i, 0)),
        scratch_shapes=[pltpu.VMEM((B_HEADS, 2 * tq, LANES), BF16), state, state, state],
        compiler_params=_cparams(("parallel", "arbitrary")),
        name="diff_attention",
    )(proj, proj, proj, lq1.reshape(1, -1), lk1.reshape(1, -1), lq2.reshape(1, -1),
      lk2.reshape(1, -1), subln.reshape(1, -1))


ODD_Q = 0
ODD_K = C_QW
ODD_V = ODD_K + C_KV_HEADS * LANES
ODD_QI = ODD_V + C_KV_HEADS * LANES
ODD_KI = ODD_QI + IDX_HEADS * IDX_DIM
ODD_WI = ODD_KI + LANES
ODD_N = ODD_WI + LANES


def _stack_heads(x, lo):
    parts = []
    for pair in range(x.shape[1] // LANES):
        xp = x[:, pair * LANES:(pair + 1) * LANES]
        parts += [jnp.where(lo, xp, 0), jnp.where(lo, 0, xp)]
    return jnp.concatenate(parts, axis=0)


WORD_BITS = 32
PLANE_ROWS = 8 * WORD_BITS
_SWAPS = ((16, 0x0000FFFF), (8, 0x00FF00FF), (4, 0x0F0F0F0F), (2, 0x33333333), (1, 0x55555555))


def _transpose32(words):
    a = list(words)
    for j, m in _SWAPS:
        for k in range(WORD_BITS):
            if k & j == 0:
                t = (lax.shift_right_logical(a[k], jnp.int32(j)) ^ a[k + j]) & jnp.int32(m)
                a[k + j] = a[k + j] ^ t
                a[k] = a[k] ^ lax.shift_left(t, jnp.int32(j))
    return a


def _index_kernel(qi_ref, ki_ref, wi_ref, bias_ref, qs_sc, p_sc, c_sc, g_sc, t_sc,
                  *, tq, tk, nkc, topk, idx_bits):
    i = pl.program_id(1)
    nv = (i * tq) // tk + 1
    ppc = tk // PLANE_ROWS
    hpd = max(1, min(IDX_HEADS, SCORE_COLS // tq))
    kf = float(topk)
    krow = lax.broadcasted_iota(jnp.int32, (tk, tq), 0)
    qpos = i * tq + lax.broadcasted_iota(jnp.int32, (tk, tq), 1)
    srow = lax.broadcasted_iota(jnp.int32, (8, tq), 0)
    qs_sc[...] = _stack_heads(qi_ref[0], _lo_lanes())
    w_t = wi_ref[0].astype(F32).T

    def score_body(j, carry):
        start = pl.multiple_of(j * tk, tk)
        k = ki_ref[0, pl.ds(start, tk), :]
        score = jnp.zeros((tk, tq), F32)
        for h0 in range(0, IDX_HEADS, hpd):
            s = _dot_nt(k, qs_sc[h0 * tq:(h0 + hpd) * tq])
            for h in range(hpd):
                score = score + jnp.maximum(s[:, h * tq:(h + 1) * tq], 0.0) * w_t[h0 + h:h0 + h + 1, :]
        score = jnp.where(start + krow <= qpos, score, -jnp.inf)
        bits = lax.bitcast_convert_type(score, jnp.int32)
        key = bits ^ (lax.shift_right_arithmetic(bits, jnp.int32(31)) | jnp.int32(-2 ** 31))
        for t in range(ppc):
            r0 = t * PLANE_ROWS
            planes = _transpose32([key[r0 + v * 8:r0 + v * 8 + 8, :] for v in range(WORD_BITS)])
            for b in range(WORD_BITS):
                p_sc[b, j * ppc + t] = planes[b]
        return carry

    lax.fori_loop(0, nv, score_body, 0)

    def each_tile(fn, init):
        def body(j, carry):
            for t in range(ppc):
                carry = fn(j * ppc + t, carry)
            return carry
        return lax.fori_loop(0, nv, body, init)

    def lane_count(word_fn):
        acc = each_tile(lambda pv, a: a + lax.population_count(word_fn(pv)), jnp.zeros((8, tq), jnp.int32))
        return jnp.sum(acc.astype(F32), axis=0, keepdims=True)

    c_sc[...] = jnp.full(c_sc.shape, -1, jnp.int32)
    g_sc[...] = jnp.zeros(g_sc.shape, jnp.int32)

    def bit_body(t, above):
        b = WORD_BITS - 1 - t
        ones = lane_count(lambda pv: c_sc[pv] & p_sc[b, pv])
        take = above + ones >= kf
        takem = jnp.where(take, -1, 0).astype(jnp.int32)

        def update(pv, carry):
            c = c_sc[pv]
            hi = c & p_sc[b, pv]
            c_sc[pv] = (hi & takem) | ((c ^ hi) & ~takem)
            g_sc[pv] = g_sc[pv] | (hi & ~takem)
            return carry

        each_tile(update, 0)
        return jnp.where(take, above, above + ones)

    above = lax.fori_loop(0, WORD_BITS, bit_body, jnp.zeros((1, tq), F32))

    def below(pv, c):
        nb = jnp.clip(lax.shift_right_arithmetic(c - pv * PLANE_ROWS - srow + 7, jnp.int32(3)), 0, WORD_BITS)
        return jnp.where(nb >= WORD_BITS, -1, lax.shift_left(jnp.int32(1), nb) - 1)

    t_sc[...] = jnp.full(t_sc.shape, -1, jnp.int32)
    need = kf - above
    n_tie = lane_count(lambda pv: c_sc[pv])

    @pl.when(jnp.max(n_tie - need) > 0.0)
    def _():
        def idx_body(t, cur):
            cand = cur | lax.shift_left(jnp.int32(1), idx_bits - 1 - t)
            cnt = lane_count(lambda pv: c_sc[pv] & below(pv, cand))
            return jnp.where(cnt < need, cand, cur)

        cut = lax.fori_loop(0, idx_bits, idx_body, jnp.zeros((1, tq), jnp.int32))

        def keep(pv, carry):
            t_sc[pv] = below(pv, cut + 1)
            return carry

        each_tile(keep, 0)

    for j in range(nkc):
        @pl.when(j < nv)
        def _():
            pieces = []
            for t in range(ppc):
                pv = j * ppc + t
                sel = g_sc[pv] | (c_sc[pv] & t_sc[pv])
                pieces += [lax.shift_right_logical(sel, jnp.int32(v)) & 1 for v in range(WORD_BITS)]
            sel = jnp.concatenate(pieces, axis=0)
            ok = (sel != 0) & (j * tk + krow <= qpos)
            bias_ref[0, j] = jnp.where(ok, 0.0, NEG).T.astype(bias_ref.dtype)

        @pl.when(j >= nv)
        def _():
            bias_ref[0, j] = jnp.full((tq, tk), NEG, bias_ref.dtype)


def _dsa_select(proj, topk, tq=ATTN_ROWS, tk=ATTN_KEYS):
    b, s, _ = proj.shape
    nkc = s // tk
    idx_bits = max(1, (s - 1).bit_length())
    qiw = IDX_HEADS * IDX_DIM
    words = pltpu.VMEM((s // PLANE_ROWS, 8, tq), jnp.int32)
    return pl.pallas_call(
        functools.partial(_index_kernel, tq=tq, tk=tk, nkc=nkc, topk=topk, idx_bits=idx_bits),
        out_shape=jax.ShapeDtypeStruct((b, nkc, s, tk), BF16),
        grid=(b, s // tq),
        in_specs=[pl.BlockSpec((1, tq, qiw), lambda bi, i: (bi, i, ODD_QI // qiw)),
                  pl.BlockSpec((1, s, LANES), lambda bi, i: (bi, 0, ODD_KI // LANES)),
                  pl.BlockSpec((1, tq, LANES), lambda bi, i: (bi, i, ODD_WI // LANES))],
        out_specs=pl.BlockSpec((1, nkc, tq, tk), lambda bi, i: (bi, 0, i, 0)),
        scratch_shapes=[pltpu.VMEM((IDX_HEADS * tq, LANES), BF16),
                        pltpu.VMEM((WORD_BITS, s // PLANE_ROWS, 8, tq), jnp.int32),
                        words, words, words],
        compiler_params=_cparams(("parallel", "arbitrary")),
        name="dsa_select",
    )(proj, proj, proj)


def _dsa_kernel(q_ref, k_ref, v_ref, bias_ref, o_ref, qs_sc, m_sc, acc_sc, *, tq, tk):
    i = pl.program_id(1)
    nv = (i * tq) // tk + 1
    lo = _lo_lanes()
    gw = C_GROUP * HEAD_DIM
    for g in range(C_KV_HEADS):
        qs_sc[g] = _stack_heads(q_ref[0, :, g * gw:(g + 1) * gw], lo)
    m_sc[...] = jnp.full(m_sc.shape, NEG, F32)
    acc_sc[...] = jnp.zeros(acc_sc.shape, F32)

    def body(j, carry):
        start = pl.multiple_of(j * tk, tk)
        bias = bias_ref[0, j].astype(F32)
        bias = jnp.concatenate([bias] * C_GROUP, axis=0)
        for g in range(C_KV_HEADS):
            k = k_ref[0, pl.ds(start, tk), g * LANES:(g + 1) * LANES]
            v = jnp.where(lo, v_ref[0, pl.ds(start, tk), g * LANES:(g + 1) * LANES], 1)
            s = _dot_nt(qs_sc[g], k) + bias
            m_prev = m_sc[g]
            m_new = jnp.maximum(m_prev, jnp.max(s, axis=-1, keepdims=True))
            p = jnp.exp2(s - _rep(m_new, tk))
            m_sc[g] = m_new
            acc_sc[g] = (jnp.exp2(m_prev - m_new) * acc_sc[g]
                         + jnp.dot(p.astype(BF16), v, preferred_element_type=F32))
        return carry

    lax.fori_loop(0, nv, body, 0)
    for g in range(C_KV_HEADS):
        acc = acc_sc[g]
        swapped = pltpu.roll(acc, HEAD_DIM, 1)
        for pair in range(gw // LANES):
            r = 2 * pair * tq
            col = g * gw + pair * LANES
            even = acc[r:r + tq] / swapped[r:r + tq]
            odd = swapped[r + tq:r + 2 * tq] / acc[r + tq:r + 2 * tq]
            o_ref[0, :, col:col + LANES] = jnp.where(lo, even, odd).astype(o_ref.dtype)


def _dsa_attention(proj, bias, tq=ATTN_ROWS, tk=ATTN_KEYS):
    b, s, _ = proj.shape
    nkc = s // tk
    kvw = C_KV_HEADS * LANES
    state = pltpu.VMEM((C_KV_HEADS, C_GROUP * tq, LANES), F32)
    return pl.pallas_call(
        functools.partial(_dsa_kernel, tq=tq, tk=tk),
        out_shape=jax.ShapeDtypeStruct((b, s, C_QW), BF16),
        grid=(b, s // tq),
        in_specs=[pl.BlockSpec((1, tq, C_QW), lambda bi, i: (bi, i, ODD_Q // C_QW)),
                  pl.BlockSpec((1, s, kvw), lambda bi, i: (bi, 0, ODD_K // kvw)),
                  pl.BlockSpec((1, s, kvw), lambda bi, i: (bi, 0, ODD_V // kvw)),
                  pl.BlockSpec((1, nkc, tq, tk), lambda bi, i: (bi, 0, i, 0))],
        out_specs=pl.BlockSpec((1, tq, C_QW), lambda bi, i: (bi, i, 0)),
        scratch_shapes=[pltpu.VMEM((C_KV_HEADS, C_GROUP * tq, LANES), BF16), state, state],
        compiler_params=_cparams(("parallel", "arbitrary")),
        name="dsa_attention",
    )(proj, proj, proj, bias)


def _mix_ffn_kernel(*refs, n_parts, final_norm):
    x_ref = refs[0]
    parts = refs[1:1 + 2 * n_parts]
    g_ref, wu_ref, wd_ref, gf_ref, o_ref, x1_sc, h_sc, acc_sc = refs[1 + 2 * n_parts:]
    f = pl.program_id(1)

    @pl.when(f == 0)
    def _():
        x1 = x_ref[...]
        for a_ref, w_ref in zip(parts[0::2], parts[1::2]):
            x1 = x1 + jnp.dot(a_ref[...], w_ref[...], preferred_element_type=F32)
        x1_sc[...] = x1
        h_sc[...] = _rms(x1, g_ref[...]).astype(BF16)
        acc_sc[...] = jnp.zeros(acc_sc.shape, F32)

    u = jnp.maximum(jnp.dot(h_sc[...], wu_ref[0], preferred_element_type=F32), 0.0)
    acc_sc[...] += jnp.dot((u * u).astype(BF16), wd_ref[...], preferred_element_type=F32)

    @pl.when(f == pl.num_programs(1) - 1)
    def _():
        y = x1_sc[...] + acc_sc[...]
        o_ref[...] = _rms(y, gf_ref[...]) if final_norm else y


def _mix_ffn(x2, parts, g, wu, wd, g_final, final_norm, tm=FFN_ROWS, tf=FFN_HIDDEN):
    m_rows, d = x2.shape
    tm = min(tm, m_rows)
    dff = wu.shape[1]
    row = lambda cols: pl.BlockSpec((tm, cols), lambda i, f: (i, 0))
    const = lambda shape: pl.BlockSpec(shape, lambda i, f: (0, 0))
    in_specs = [row(d)]
    args = [x2]
    for a, w in parts:
        in_specs += [row(a.shape[1]), const(w.shape)]
        args += [a, w]
    in_specs += [const((1, d)),
                 pl.BlockSpec((1, d, tf), lambda i, f: (f, 0, 0)),
                 pl.BlockSpec((tf, d), lambda i, f: (f, 0)),
                 const((1, d))]
    wu_tiles = wu.reshape(d, dff // tf, tf).transpose(1, 0, 2)
    args += [g.reshape(1, d), wu_tiles, wd, g_final.reshape(1, d)]
    return pl.pallas_call(
        functools.partial(_mix_ffn_kernel, n_parts=len(parts), final_norm=final_norm),
        out_shape=jax.ShapeDtypeStruct((m_rows, d), F32),
        grid=(m_rows // tm, dff // tf),
        in_specs=in_specs,
        out_specs=row(d),
        scratch_shapes=[pltpu.VMEM((tm, d), F32), pltpu.VMEM((tm, d), BF16), pltpu.VMEM((tm, d), F32)],
        compiler_params=_cparams(("parallel", "arbitrary")),
        name="mix_ffn",
    )(*args)


Q_SCALE = HEAD_DIM ** -0.5 * math.log2(math.e)


def _even_chunks():
    scale = Q_SCALE
    w = A_W
    return ((0, w, True, scale, 1, 0), (w, w, True, 1.0, 1, w), (2 * w, w, False, 1.0, 1, 2 * w),
            (3 * w, w, True, scale, 0, 0), (4 * w, w, True, 1.0, 0, w), (5 * w, w, False, 1.0, 0, 2 * w))


def _odd_chunks():
    w = 512
    chunks = ((ODD_Q, w, True, Q_SCALE), (ODD_Q + w, w, True, Q_SCALE),
              (ODD_K, w, True, 1.0), (ODD_V, w, False, 1.0),
              (ODD_QI, w, True, IDX_DIM ** -0.5), (ODD_KI, LANES, True, 1.0),
              (ODD_WI, LANES, False, IDX_HEADS ** -0.5))
    return tuple(c + (0, c[0]) for c in chunks)


def _odd_weight(w):
    d = w.shape[0]
    q = w[:, :C_QW]
    k = w[:, C_QW:C_QW + C_KVW].reshape(d, C_KV_HEADS, 1, HEAD_DIM)
    v = w[:, C_QW + C_KVW:C_QW + 2 * C_KVW].reshape(d, C_KV_HEADS, 1, HEAD_DIM)
    dup = lambda t: jnp.broadcast_to(t, (d, C_KV_HEADS, 2, HEAD_DIM)).reshape(d, C_KV_HEADS * LANES)
    o = C_QW + 2 * C_KVW
    qi = w[:, o:o + IDX_HEADS * IDX_DIM]
    ki = w[:, o + IDX_HEADS * IDX_DIM:o + IDX_HEADS * IDX_DIM + IDX_DIM]
    wi = w[:, o + IDX_HEADS * IDX_DIM + IDX_DIM:]
    pad = jnp.zeros((d, LANES - IDX_HEADS), w.dtype)
    return jnp.concatenate([q, dup(k), dup(v), qi, ki, ki, wi, pad], axis=1)


def kernel(x, norm_mix, norm_ffn, w_in_even, w_out_even, lambda_q1, lambda_k1, lambda_q2,
           lambda_k2, diff_subln, w_in_odd, w_out_odd, w_ffn_up, w_ffn_down, norm_final):
    b, s, d = x.shape
    depth = norm_mix.shape[0]
    tables = _rope_tables(s)
    topk = min(TOPK_MAX, s // 4)
    x2 = x.reshape(b * s, d)
    for layer in range(depth):
        if layer % 2 == 0:
            e = layer // 2
            proj_b, proj_a = _norm_proj(x2, norm_mix[layer], w_in_even[e].astype(BF16), tables,
                                        _even_chunks(), s, ((3 * B_W, BF16), (3 * A_W, F32)))
            out_a = _dilated_attention(proj_a.reshape(b, s, 3 * A_W))
            lam_init = 0.8 - 0.6 * math.exp(-0.3 * layer)
            out_b = _diff_attention(proj_b.reshape(b, s, 3 * B_W), lambda_q1[e], lambda_k1[e],
                                    lambda_q2[e], lambda_k2[e], diff_subln[e], lam_init)
            wo = w_out_even[e].astype(BF16)
            parts = [(out_a.reshape(b * s, A_W), wo[:A_W]), (out_b.reshape(b * s, B_W), wo[A_W:])]
        else:
            o = layer // 2
            (proj,) = _norm_proj(x2, norm_mix[layer], _odd_weight(w_in_odd[o]).astype(BF16), tables,
                                 _odd_chunks(), s, ((ODD_N, BF16),))
            proj3 = proj.reshape(b, s, proj.shape[1])
            bias = _dsa_select(proj3, topk)
            out_c = _dsa_attention(proj3, bias)
            parts = [(out_c.reshape(b * s, C_QW), w_out_odd[o].astype(BF16))]
        x2 = _mix_ffn(x2, parts, norm_ffn[layer], w_ffn_up[layer].astype(BF16),
                      w_ffn_down[layer].astype(BF16), norm_final, layer == depth - 1)
    return x2.reshape(b, s, d)
```
